```python
import jax, jax.numpy as jnp
from jax import lax
import numpy as np

D_MODEL = 4096
BATCH = 4
SEQ = 4096
DEPTH = 1

MIX_WIDTH = D_MODEL
M_HEADS = 4
M_DV = MIX_WIDTH // 2 // M_HEADS
M_DK = M_DV // 2
G_HEADS = 4
G_DV = MIX_WIDTH // 2 // G_HEADS
G_DK = G_DV // 2
G_RANK = 16
G_TAU = 16.0
QK_CONV = 4
FFN_CONV = 3
D_FF = ((8 * D_MODEL // 3) + 127) // 128 * 128
CHUNK = 64
EPS = 1e-6

IN_SPLITS = (
    M_HEADS * M_DK,
    M_HEADS * M_DK,
    M_HEADS * M_DV,
    M_HEADS * M_DV,
    M_HEADS,
    M_HEADS,
    G_HEADS * G_DK,
    G_HEADS * G_DK,
    G_HEADS * G_DV,
    G_HEADS * G_DV,
    G_RANK,
)
D_IN = sum(IN_SPLITS)

kernel_name = "hymba_mlstm_gla_convffn"


def rmsnorm(x, g):
    xf = x.astype(jnp.float32)
    xf = xf * lax.rsqrt(jnp.mean(xf * xf, axis=-1, keepdims=True) + EPS)
    return (xf * g.astype(jnp.float32)).astype(x.dtype)


def head_rmsnorm(h, g):
    B, S, H, Dv = h.shape
    h = h * lax.rsqrt(jnp.mean(h * h, axis=-1, keepdims=True) + EPS)
    return h.reshape(B, S, H * Dv) * g.astype(jnp.float32)


def causal_dwconv(x, w, b):
    K, C = w.shape
    y = lax.conv_general_dilated(
        x, w[:, None, :].astype(x.dtype), window_strides=(1,), padding=[(K - 1, 0)],
        dimension_numbers=("NWC", "WIO", "NWC"), feature_group_count=C)
    return y + b.astype(x.dtype)


def to_chunks(a):
    B, S, H = a.shape[:3]
    a = a.reshape(B, S // CHUNK, CHUNK, H, *a.shape[3:])
    return jnp.moveaxis(a, (1, 3), (0, 2))


def from_chunks(a):
    NC, B, H, L, D = a.shape
    return jnp.moveaxis(a, (0, 2), (1, 3)).reshape(B, NC * L, H, D)


def mlstm_chunkwise(q, k, v, i_pre, f_pre):
    f32 = jnp.float32
    B, S, H, DK = q.shape
    DV = v.shape[-1]
    qc = to_chunks(q.astype(f32) * (DK ** -0.5))
    kc = to_chunks(k.astype(f32))
    vc = to_chunks(v.astype(f32))
    ig = to_chunks(i_pre.astype(f32))
    bcum = jnp.cumsum(to_chunks(jax.nn.log_sigmoid(f_pre.astype(f32))), axis=-1)
    causal = jnp.tril(jnp.ones((CHUNK, CHUNK), dtype=bool))

    def step(carry, inp):
        C, n, m = carry
        q_, k_, v_, b, i_ = inp
        D = jnp.where(causal, b[..., :, None] - b[..., None, :] + i_[..., None, :], -jnp.inf)
        inter = b + m[..., None]
        m_t = jnp.maximum(inter, jnp.max(D, axis=-1))
        g = jnp.exp(inter - m_t)
        s_qk = jnp.einsum("bhtd,bhsd->bhts", q_, k_) * jnp.exp(D - m_t[..., None])
        num = g[..., None] * jnp.einsum("bhtd,bhde->bhte", q_, C) + jnp.einsum("bhts,bhse->bhte", s_qk, v_)
        den = g * jnp.einsum("bhtd,bhd->bht", q_, n) + jnp.sum(s_qk, axis=-1)
        h = num / jnp.maximum(jnp.abs(den), jnp.exp(-m_t))[..., None]
        bL = b[..., -1]
        a_s = bL[..., None] - b + i_
        m_new = jnp.maximum(bL + m, jnp.max(a_s, axis=-1))
        ws = jnp.exp(a_s - m_new[..., None])
        gC = jnp.exp(bL + m - m_new)
        C = gC[..., None, None] * C + jnp.einsum("bhs,bhsd,bhse->bhde", ws, k_, v_)
        n = gC[..., None] * n + jnp.einsum("bhs,bhsd->bhd", ws, k_)
        return (C, n, m_new), h

    init = (jnp.zeros((B, H, DK, DV), f32), jnp.zeros((B, H, DK), f32), jnp.zeros((B, H), f32))
    _, h = lax.scan(step, init, (qc, kc, vc, bcum, ig))
    return from_chunks(h)


def gla_chunkwise(q, k, v, log_a):
    f32 = jnp.float32
    B, S, H, DK = q.shape
    DV = v.shape[-1]
    qc = to_chunks(q.astype(f32) * (DK ** -0.5))
    kc = to_chunks(k.astype(f32))
    vc = to_chunks(v.astype(f32))
    bcum = jnp.cumsum(to_chunks(log_a.astype(f32)), axis=3)
    causal = jnp.tril(jnp.ones((CHUNK, CHUNK), dtype=bool))[..., None]

    def step(Sst, inp):
        q_, k_, v_, b = inp
        inter = jnp.einsum("bhtd,bhde->bhte", q_ * jnp.exp(b), Sst)
        diff = b[..., :, None, :] - b[..., None, :, :]
        decay = jnp.exp(jnp.where(causal, diff, -jnp.inf))
        A = jnp.einsum("bhtd,bhsd,bhtsd->bhts", q_, k_, decay)
        intra = jnp.einsum("bhts,bhse->bhte", A, v_)
        bL = b[..., -1, :]
        kdec = k_ * jnp.exp(bL[..., None, :] - b)
        Sst = jnp.exp(bL)[..., None] * Sst + jnp.einsum("bhsd,bhse->bhde", kdec, v_)
        return Sst, inter + intra

    _, o = lax.scan(step, jnp.zeros((B, H, DK, DV), f32), (qc, kc, vc, bcum))
    return from_chunks(o)


def setup_inputs(seed: int = 0) -> dict:
    key = jax.random.key(seed)
    ks = jax.random.split(key, 24)
    f32 = jnp.float32
    nrm = lambda k, shape, s: jax.random.normal(k, shape, f32) * s
    return {
        "x": nrm(ks[0], (BATCH, SEQ, D_MODEL), 1.0),
        "ln1_g": 1.0 + nrm(ks[1], (D_MODEL,), 0.02),
        "w_in": nrm(ks[2], (D_MODEL, D_IN), D_MODEL ** -0.5),
        "mlstm_conv_w": nrm(ks[3], (QK_CONV, 2 * M_HEADS * M_DK), QK_CONV ** -0.5),
        "mlstm_conv_b": nrm(ks[4], (2 * M_HEADS * M_DK,), 0.01),
        "mlstm_i_b": nrm(ks[5], (M_HEADS,), 0.1),
        "mlstm_f_b": jnp.linspace(3.0, 6.0, M_HEADS, dtype=f32) + nrm(ks[6], (M_HEADS,), 0.1),
        "mlstm_norm_g": 1.0 + nrm(ks[7], (M_HEADS * M_DV,), 0.02),
        "gla_a_w2": nrm(ks[8], (G_RANK, G_HEADS * G_DK), G_RANK ** -0.5),
        "gla_a_b": nrm(ks[9], (G_HEADS * G_DK,), 0.01),
        "gla_norm_g": 1.0 + nrm(ks[10], (G_HEADS * G_DV,), 0.02),
        "w_out": nrm(ks[11], (MIX_WIDTH, D_MODEL), MIX_WIDTH ** -0.5),
        "ln2_g": 1.0 + nrm(ks[12], (D_MODEL,), 0.02),
        "w_ffn_gate": nrm(ks[13], (D_MODEL, D_FF), D_MODEL ** -0.5),
        "w_ffn_up": nrm(ks[14], (D_MODEL, D_FF), D_MODEL ** -0.5),
        "ffn_conv_w": nrm(ks[15], (FFN_CONV, D_FF), FFN_CONV ** -0.5),
        "ffn_conv_b": nrm(ks[16], (D_FF,), 0.01),
        "w_ffn_down": nrm(ks[17], (D_FF, D_MODEL), D_FF ** -0.5),
        "lnf_g": 1.0 + nrm(ks[18], (D_MODEL,), 0.02),
    }


def reference(x, ln1_g, w_in, mlstm_conv_w, mlstm_conv_b, mlstm_i_b, mlstm_f_b, mlstm_norm_g,
              gla_a_w2, gla_a_b, gla_norm_g, w_out, ln2_g, w_ffn_gate, w_ffn_up,
              ffn_conv_w, ffn_conv_b, w_ffn_down, lnf_g):
    B, S, _ = x.shape
    split_idx = np.cumsum(IN_SPLITS)[:-1].tolist()
    for _layer in range(DEPTH):
        h = rmsnorm(x, ln1_g)
        proj = h @ w_in
        mq, mk, mv, mo, mi, mf, gq, gk, gv, gg, ga = jnp.split(proj, split_idx, axis=-1)

        mqk = jax.nn.silu(causal_dwconv(jnp.concatenate([mq, mk], axis=-1), mlstm_conv_w, mlstm_conv_b))
        mq, mk = jnp.split(mqk, 2, axis=-1)
        hm = mlstm_chunkwise(mq.reshape(B, S, M_HEADS, M_DK), mk.reshape(B, S, M_HEADS, M_DK),
                             mv.reshape(B, S, M_HEADS, M_DV), mi + mlstm_i_b, mf + mlstm_f_b)
        ym = head_rmsnorm(hm, mlstm_norm_g) * jax.nn.sigmoid(mo.astype(jnp.float32))

        log_a = jax.nn.log_sigmoid((ga @ gla_a_w2 + gla_a_b).astype(jnp.float32)) / G_TAU
        hg = gla_chunkwise(gq.reshape(B, S, G_HEADS, G_DK), gk.reshape(B, S, G_HEADS, G_DK),
                           gv.reshape(B, S, G_HEADS, G_DV), log_a.reshape(B, S, G_HEADS, G_DK))
        yg = head_rmsnorm(hg, gla_norm_g) * jax.nn.silu(gg.astype(jnp.float32))

        mix = jnp.concatenate([ym, yg], axis=-1).astype(x.dtype)
        x = x + mix @ w_out

        h2 = rmsnorm(x, ln2_g)
        gate = causal_dwconv(h2 @ w_ffn_gate, ffn_conv_w, ffn_conv_b)
        x = x + (jax.nn.silu(gate) * (h2 @ w_ffn_up)) @ w_ffn_down
    return rmsnorm(x, lnf_g)
```

```python
import functools

import jax
import jax.numpy as jnp
from jax import lax
from jax.experimental import pallas as pl
from jax.experimental.pallas import tpu as pltpu

F32 = jnp.float32
BF16 = jnp.bfloat16

HEADS = 4
QK_CONV = 4
FFN_CONV = 3
G_RANK = 16
G_TAU_INV = 1.0 / 16.0
EPS = 1e-6
SMALL_W = 128
HALO = 8
NEG_INF = float("-inf")

VMEM_LIMIT_V7X = 56 * 1024 * 1024


def _cparams(sem, vmem=VMEM_LIMIT_V7X):
    return pltpu.CompilerParams(dimension_semantics=sem, vmem_limit_bytes=vmem)


def _pick(n, prefs):
    for p in prefs:
        if n % p == 0:
            return p
    raise ValueError(f"no tile in {prefs} divides {n}")


def _log_sigmoid(x):
    return jnp.minimum(x, 0.0) - jnp.log1p(jnp.exp(-jnp.abs(x)))


def _dot(a, b):
    return jnp.dot(a, b, preferred_element_type=F32)


def _dot_nt(a, b):
    return lax.dot_general(a, b, (((1,), (1,)), ((), ())), preferred_element_type=F32)


def _dot_tn(a, b):
    return lax.dot_general(a, b, (((0,), (0,)), ((), ())), preferred_element_type=F32)


def _rmsnorm_kernel(x_ref, g_ref, o_ref):
    x = x_ref[...].astype(F32)
    y = x * lax.rsqrt(jnp.mean(x * x, axis=-1, keepdims=True) + EPS)
    o_ref[...] = (y * g_ref[...]).astype(o_ref.dtype)


def _rmsnorm(x, g, out_dtype):
    T, D = x.shape
    tr = _pick(T, (256, 128, 64, 8))
    return pl.pallas_call(
        _rmsnorm_kernel,
        grid=(T // tr,),
        in_specs=[pl.BlockSpec((tr, D), lambda i: (i, 0)),
                  pl.BlockSpec((1, D), lambda i: (0, 0))],
        out_specs=pl.BlockSpec((tr, D), lambda i: (i, 0)),
        out_shape=jax.ShapeDtypeStruct((T, D), out_dtype),
        compiler_params=_cparams(("parallel",)),
        name="rmsnorm",
    )(x, g.reshape(1, D).astype(F32))


def _matmul_kernel(a_ref, w_ref, o_ref):
    o_ref[...] = _dot(a_ref[...], w_ref[...]).astype(o_ref.dtype)


def _matmul(a, w, out_dtype, name):
    M, K = a.shape
    N = w.shape[1]
    tm = _pick(M, (1024, 512, 256, 128))
    tn = _pick(N, (1024, 512, 256, 128))
    return pl.pallas_call(
        _matmul_kernel,
        grid=(M // tm, N // tn),
        in_specs=[pl.BlockSpec((tm, K), lambda i, j: (i, 0)),
                  pl.BlockSpec((K, tn), lambda i, j: (0, j))],
        out_specs=pl.BlockSpec((tm, tn), lambda i, j: (i, j)),
        out_shape=jax.ShapeDtypeStruct((M, N), out_dtype),
        compiler_params=_cparams(("parallel", "arbitrary")),
        name=name,
    )(a, w)


def _outproj_kernel(ym_ref, yg_ref, wt_ref, wb_ref, x_ref, o_ref):
    acc = _dot(ym_ref[...], wt_ref[...]) + _dot(yg_ref[...], wb_ref[...])
    o_ref[...] = x_ref[...] + acc


def _outproj(ym, yg, w_top, w_bot, x):
    M, K2 = ym.shape
    N = w_top.shape[1]
    tm = _pick(M, (1024, 512, 256, 128))
    tn = _pick(N, (1024, 512, 256, 128))
    return pl.pallas_call(
        _outproj_kernel,
        grid=(M // tm, N // tn),
        in_specs=[pl.BlockSpec((tm, K2), lambda i, j: (i, 0)),
                  pl.BlockSpec((tm, K2), lambda i, j: (i, 0)),
                  pl.BlockSpec((K2, tn), lambda i, j: (0, j)),
                  pl.BlockSpec((K2, tn), lambda i, j: (0, j)),
                  pl.BlockSpec((tm, tn), lambda i, j: (i, j))],
        out_specs=pl.BlockSpec((tm, tn), lambda i, j: (i, j)),
        out_shape=jax.ShapeDtypeStruct((M, N), F32),
        compiler_params=_cparams(("parallel", "arbitrary")),
        name="outproj",
    )(ym, yg, w_top, w_bot, x)


def _conv_silu(raw_ref, x_ref, w_ref, b_ref, L):
    x_ref[HALO:HALO + L, :] = raw_ref[...].astype(F32)
    acc = b_ref[...]
    for j in range(QK_CONV):
        off = HALO - (QK_CONV - 1) + j
        acc = acc + w_ref[j:j + 1, :] * x_ref[off:off + L, :]
    x_ref[0:HALO, :] = x_ref[L:L + HALO, :]
    return acc * jax.nn.sigmoid(acc)


def _row_to_col(row, eye):
    L = eye.shape[0]
    return jnp.sum(jnp.where(eye, jnp.broadcast_to(row, (L, L)), 0.0), axis=1, keepdims=True)


def _mlstm_kernel(bias_ref, q_ref, k_ref, v_ref, o_ref, g_ref, cwq_ref, cwk_ref, cbq_ref,
                  cbk_ref, ng_ref, out_ref, C_ref, n_ref, m_ref, xq_ref, xk_ref, *, L, DK, DV):
    h = pl.program_id(1)
    c = pl.program_id(2)

    @pl.when(c == 0)
    def _():
        C_ref[...] = jnp.zeros_like(C_ref)
        n_ref[...] = jnp.zeros_like(n_ref)
        m_ref[...] = jnp.zeros_like(m_ref)
        xq_ref[0:HALO, :] = jnp.zeros((HALO, DK), F32)
        xk_ref[0:HALO, :] = jnp.zeros((HALO, DK), F32)

    q = _conv_silu(q_ref, xq_ref, cwq_ref, cbq_ref, L) * (DK ** -0.5)
    k = _conv_silu(k_ref, xk_ref, cwk_ref, cbk_ref, L)
    v = v_ref[...]

    gi = g_ref[0:1, :] + bias_ref[0, h]
    lf = _log_sigmoid(g_ref[1:2, :] + bias_ref[1, h])

    lane8 = lax.broadcasted_iota(jnp.int32, (8, L), 1)
    bc = jnp.broadcast_to(lf, (8, L))
    s = 1
    while s < L:
        bc = bc + jnp.where(lane8 >= s, pltpu.roll(bc, s, axis=1), 0.0)
        s *= 2
    b_row = bc[0:1, :]

    row = lax.broadcasted_iota(jnp.int32, (L, L), 0)
    col = lax.broadcasted_iota(jnp.int32, (L, L), 1)
    eye = row == col
    b_col = _row_to_col(b_row, eye)

    m_prev = m_ref[...]
    D = jnp.where(col <= row, b_col - b_row + gi, NEG_INF)
    inter = b_col + m_prev
    m_t = jnp.maximum(inter, jnp.max(D, axis=1, keepdims=True))
    g = jnp.exp(inter - m_t)
    qb = q.astype(BF16)
    kb = k.astype(BF16)
    s_qk = _dot_nt(qb, kb) * jnp.exp(D - m_t)
    num = g * _dot(qb, C_ref[...].astype(BF16)) + _dot(s_qk.astype(BF16), v)
    den = (g * jnp.sum(q * n_ref[...], axis=1, keepdims=True)
           + jnp.sum(s_qk, axis=1, keepdims=True))
    hh = num / jnp.maximum(jnp.abs(den), jnp.exp(-m_t))

    hn = hh * lax.rsqrt(jnp.mean(hh * hh, axis=1, keepdims=True) + EPS)
    y = hn * ng_ref[...] * jax.nn.sigmoid(o_ref[...].astype(F32))
    out_ref[...] = y.astype(out_ref.dtype)

    bL = b_row[:, L - 1:L]
    a_row = bL - b_row + gi
    m_new = jnp.maximum(bL + m_prev, jnp.max(a_row, axis=1, keepdims=True))
    ws_col = _row_to_col(jnp.exp(a_row - m_new), eye)
    gC = jnp.exp(bL + m_prev - m_new)
    kw = k * ws_col
    C_ref[...] = gC * C_ref[...] + _dot_tn(kw.astype(BF16), v)
    n_ref[...] = gC * n_ref[...] + jnp.sum(kw, axis=0, keepdims=True)
    m_ref[...] = m_new


def _mlstm(proj, gates, bias, conv_w, conv_b, norm_g, *, B, S, DK, DV, L):
    H = HEADS
    NC = S // L
    T = B * S
    kern = functools.partial(_mlstm_kernel, L=L, DK=DK, DV=DV)
    rowblk = lambda b, h, c: b * NC + c
    return pl.pallas_call(
        kern,
        grid=(B, H, NC),
        in_specs=[
            pl.BlockSpec(memory_space=pltpu.SMEM),
            pl.BlockSpec((L, DK), lambda b, h, c: (rowblk(b, h, c), h)),
            pl.BlockSpec((L, DK), lambda b, h, c: (rowblk(b, h, c), H + h)),
            pl.BlockSpec((L, DV), lambda b, h, c: (rowblk(b, h, c), H + h)),
            pl.BlockSpec((L, DV), lambda b, h, c: (rowblk(b, h, c), 2 * H + h)),
            pl.BlockSpec((None, None, None, 2, L), lambda b, h, c: (b, h, c, 0, 0)),
            pl.BlockSpec((QK_CONV, DK), lambda b, h, c: (0, h)),
            pl.BlockSpec((QK_CONV, DK), lambda b, h, c: (0, H + h)),
            pl.BlockSpec((1, DK), lambda b, h, c: (0, h)),
            pl.BlockSpec((1, DK), lambda b, h, c: (0, H + h)),
            pl.BlockSpec((1, DV), lambda b, h, c: (0, h)),
        ],
        out_specs=pl.BlockSpec((L, DV), lambda b, h, c: (rowblk(b, h, c), h)),
        out_shape=jax.ShapeDtypeStruct((T, H * DV), BF16),
        scratch_shapes=[
            pltpu.VMEM((DK, DV), F32),
            pltpu.VMEM((1, DK), F32),
            pltpu.VMEM((1, 1), F32),
            pltpu.VMEM((L + HALO, DK), F32),
            pltpu.VMEM((L + HALO, DK), F32),
        ],
        compiler_params=_cparams(("parallel", "parallel", "arbitrary")),
        name="mlstm_scan",
    )(bias, proj, proj, proj, proj, gates, conv_w, conv_w, conv_b, conv_b, norm_g)


def _gla_kernel(q_ref, k_ref, v_ref, gg_ref, sm_ref, w2_ref, ab_ref, ng_ref, out_ref,
                ST_ref, bs_ref, *, L, DK, DV):
    c = pl.program_id(2)

    @pl.when(c == 0)
    def _():
        ST_ref[...] = jnp.zeros_like(ST_ref)
        bs_ref[...] = jnp.zeros_like(bs_ref)

    z = jnp.dot(sm_ref[...], w2_ref[...], preferred_element_type=F32,
                precision=lax.Precision.HIGHEST) + ab_ref[...]
    la = _log_sigmoid(z) * G_TAU_INV
    row = lax.broadcasted_iota(jnp.int32, (L, L), 0)
    col = lax.broadcasted_iota(jnp.int32, (L, L), 1)
    tril = jnp.where(col <= row, 1.0, 0.0).astype(F32)
    b = jnp.dot(tril, la, preferred_element_type=F32, precision=lax.Precision.HIGHEST)

    q = q_ref[...].astype(F32) * (DK ** -0.5)
    k = k_ref[...].astype(F32)
    v = v_ref[...]

    ST = ST_ref[...]
    o = _dot_nt((q * jnp.exp(b)).astype(BF16), ST.astype(BF16))

    bs_ref[HALO:HALO + L, :] = b
    pos_col = lax.broadcasted_iota(jnp.int32, (L, DK), 0)
    A = jnp.where(row == col, _dot_nt(q.astype(BF16), k.astype(BF16)), 0.0)
    n = L
    while n >= 2:
        hn = n // 2
        pos = pos_col % n
        if n >= 8:
            R = jnp.concatenate(
                [jnp.broadcast_to(b[i * n + hn - 1:i * n + hn, :], (n, DK)) for i in range(L // n)],
                axis=0)
        elif n == 4:
            R = jnp.where(pos == 0, bs_ref[HALO + 1:HALO + 1 + L, :],
                          jnp.where(pos == 1, b,
                                    jnp.where(pos == 2, bs_ref[HALO - 1:HALO - 1 + L, :],
                                              bs_ref[HALO - 2:HALO - 2 + L, :])))
        else:
            R = jnp.where(pos == 0, b, bs_ref[HALO - 1:HALO - 1 + L, :])
        qn = q * jnp.exp(jnp.where(pos >= hn, b - R, NEG_INF))
        kn = k * jnp.exp(jnp.where(pos < hn, R - b, NEG_INF))
        S = _dot_nt(qn.astype(BF16), kn.astype(BF16))
        A = A + jnp.where((row // n) == (col // n), S, 0.0)
        n = hn
    o = o + _dot(A.astype(BF16), v)

    hnorm = o * lax.rsqrt(jnp.mean(o * o, axis=1, keepdims=True) + EPS)
    gg = gg_ref[...].astype(F32)
    y = hnorm * ng_ref[...] * (gg * jax.nn.sigmoid(gg))
    out_ref[...] = y.astype(out_ref.dtype)

    bL = b[L - 1:L, :]
    kdec = k * jnp.exp(bL - b)
    ST_ref[...] = ST * jnp.exp(bL) + _dot_tn(v, kdec.astype(BF16))


def _gla(proj, small, w2p, a_b, norm_g, *, B, S, DK, DV, L, col0_k, col0_v):
    H = HEADS
    NC = S // L
    T = B * S
    kern = functools.partial(_gla_kernel, L=L, DK=DK, DV=DV)
    rowblk = lambda b, h, c: b * NC + c
    return pl.pallas_call(
        kern,
        grid=(B, H, NC),
        in_specs=[
            pl.BlockSpec((L, DK), lambda b, h, c: (rowblk(b, h, c), col0_k + h)),
            pl.BlockSpec((L, DK), lambda b, h, c: (rowblk(b, h, c), col0_k + H + h)),
            pl.BlockSpec((L, DV), lambda b, h, c: (rowblk(b, h, c), col0_v + h)),
            pl.BlockSpec((L, DV), lambda b, h, c: (rowblk(b, h, c), col0_v + H + h)),
            pl.BlockSpec((L, SMALL_W), lambda b, h, c: (rowblk(b, h, c), 0)),
            pl.BlockSpec((SMALL_W, DK), lambda b, h, c: (0, h)),
            pl.BlockSpec((1, DK), lambda b, h, c: (0, h)),
            pl.BlockSpec((1, DV), lambda b, h, c: (0, h)),
        ],
        out_specs=pl.BlockSpec((L, DV), lambda b, h, c: (rowblk(b, h, c), h)),
        out_shape=jax.ShapeDtypeStruct((T, H * DV), BF16),
        scratch_shapes=[
            pltpu.VMEM((DV, DK), F32),
            pltpu.VMEM((L + 2 * HALO, DK), F32),
        ],
        compiler_params=_cparams(("parallel", "parallel", "arbitrary")),
        name="gla_scan",
    )(proj, proj, proj, proj, small, w2p, a_b, norm_g)


def _ffn_kernel(h_ref, halo_ref, wg_ref, wu_ref, cw_ref, cb_ref, wd_ref, x_ref, gf_ref,
                out_ref, gbuf_ref, *, tm, S, tn):
    i = pl.program_id(0)
    f = pl.program_id(1)
    nf = pl.num_programs(1)
    D = out_ref.shape[1]

    @pl.when(f == 0)
    def _():
        out_ref[...] = x_ref[...]

    hb = h_ref[...]
    gate = _dot(hb, wg_ref[...])
    up = _dot(hb, wu_ref[...])
    seq_start = (i * tm) % S == 0
    ghalo = _dot(halo_ref[...], wg_ref[...])
    gbuf_ref[0:HALO, :] = jnp.where(seq_start, 0.0, ghalo)
    gbuf_ref[HALO:HALO + tm, :] = gate
    conv = cb_ref[...] + cw_ref[FFN_CONV - 1:FFN_CONV, :] * gate
    for j in range(FFN_CONV - 1):
        off = HALO - (FFN_CONV - 1) + j
        conv = conv + cw_ref[j:j + 1, :] * gbuf_ref[off:off + tm, :]
    act = ((conv * jax.nn.sigmoid(conv)) * up).astype(BF16)
    for n0 in range(0, D, tn):
        out_ref[:, n0:n0 + tn] += _dot(act, wd_ref[:, n0:n0 + tn])

    @pl.when(f == nf - 1)
    def _():
        x2 = out_ref[...]
        y = x2 * lax.rsqrt(jnp.mean(x2 * x2, axis=-1, keepdims=True) + EPS)
        out_ref[...] = y * gf_ref[...]


def _ffn(h2, wg, wu, conv_w, conv_b, wd, x1, lnf_g, *, S):
    T, D = h2.shape
    F = wg.shape[1]
    tm = _pick(S, (512, 256, 128))
    tf = _pick(F, (256, 128))
    tn = _pick(D, (1024, 512, 256, 128))
    kern = functools.partial(_ffn_kernel, tm=tm, S=S, tn=tn)
    hb = tm // HALO
    once = pl.Buffered(1)
    return pl.pallas_call(
        kern,
        grid=(T // tm, F // tf),
        in_specs=[
            pl.BlockSpec((tm, D), lambda i, f: (i, 0), pipeline_mode=once),
            pl.BlockSpec((HALO, D), lambda i, f: (jnp.maximum(i * hb - 1, 0), 0)),
            pl.BlockSpec((D, tf), lambda i, f: (0, f)),
            pl.BlockSpec((D, tf), lambda i, f: (0, f)),
            pl.BlockSpec((FFN_CONV, tf), lambda i, f: (0, f)),
            pl.BlockSpec((1, tf), lambda i, f: (0, f)),
            pl.BlockSpec((tf, D), lambda i, f: (f, 0)),
            pl.BlockSpec((tm, D), lambda i, f: (i, 0), pipeline_mode=once),
            pl.BlockSpec((1, D), lambda i, f: (0, 0)),
        ],
        out_specs=pl.BlockSpec((tm, D), lambda i, f: (i, 0)),
        out_shape=jax.ShapeDtypeStruct((T, D), F32),
        scratch_shapes=[pltpu.VMEM((tm + HALO, tf), F32)],
        compiler_params=_cparams(("parallel", "arbitrary")),
        name="convffn",
    )(h2, h2, wg, wu, conv_w, conv_b, wd, x1, lnf_g)


def kernel(x, ln1_g, w_in, mlstm_conv_w, mlstm_conv_b, mlstm_i_b, mlstm_f_b, mlstm_norm_g,
           gla_a_w2, gla_a_b, gla_norm_g, w_out, ln2_g, w_ffn_gate, w_ffn_up,
           ffn_conv_w, ffn_conv_b, w_ffn_down, lnf_g):
    B, S, D = x.shape
    T = B * S
    H = HEADS
    M_DK = mlstm_conv_w.shape[1] // (2 * H)
    M_DV = mlstm_norm_g.shape[0] // H
    G_DK = gla_a_w2.shape[1] // H
    G_DV = gla_norm_g.shape[0] // H
    F = w_ffn_gate.shape[1]
    assert M_DV == 2 * M_DK and G_DV == 2 * G_DK and M_DK == G_DK

    n_m = 2 * H * M_DK + 2 * H * M_DV
    n_g = 2 * H * G_DK + 2 * H * G_DV
    o_gate = n_m
    o_g = n_m + 2 * H
    o_a = o_g + n_g
    assert w_in.shape[1] == o_a + G_RANK

    w_big = jnp.concatenate([w_in[:, :n_m], w_in[:, o_g:o_a]], axis=1).astype(BF16)
    w_small = jnp.concatenate(
        [w_in[:, o_gate:o_g], w_in[:, o_a:],
         jnp.zeros((D, SMALL_W - 2 * H - G_RANK), w_in.dtype)], axis=1).astype(BF16)

    x2d = x.reshape(T, D)
    h1 = _rmsnorm(x2d, ln1_g, BF16)
    proj = _matmul(h1, w_big, BF16, "inproj")
    small = _matmul(h1, w_small, F32, "inproj_small")

    L_M = 128
    gates = small[:, :2 * H].reshape(B, S // L_M, L_M, 2, H)
    gates = jnp.transpose(gates, (0, 4, 1, 3, 2))
    bias = jnp.stack([mlstm_i_b, mlstm_f_b]).astype(F32)
    ym = _mlstm(proj, gates, bias, mlstm_conv_w.astype(F32),
                mlstm_conv_b.reshape(1, -1).astype(F32), mlstm_norm_g.reshape(1, -1).astype(F32),
                B=B, S=S, DK=M_DK, DV=M_DV, L=L_M)

    L_G = 64
    w2p = jnp.zeros((SMALL_W, H * G_DK), F32).at[2 * H:2 * H + G_RANK].set(gla_a_w2.astype(F32))
    yg = _gla(proj, small, w2p, gla_a_b.reshape(1, -1).astype(F32),
              gla_norm_g.reshape(1, -1).astype(F32),
              B=B, S=S, DK=G_DK, DV=G_DV, L=L_G, col0_k=n_m // G_DK, col0_v=(n_m + 2 * H * G_DK) // G_DV)

    w_out_b = w_out.astype(BF16)
    x1 = _outproj(ym, yg, w_out_b[:H * M_DV], w_out_b[H * M_DV:], x2d)

    h2 = _rmsnorm(x1, ln2_g, BF16)
    out = _ffn(h2, w_ffn_gate.astype(BF16), w_ffn_up.astype(BF16), ffn_conv_w.astype(F32),
               ffn_conv_b.reshape(1, -1).astype(F32), w_ffn_down.astype(BF16), x1,
               lnf_g.reshape(1, -1).astype(F32), S=S)
    return out.reshape(B, S, D)
```

```python
import functools

import jax
import jax.numpy as jnp
from jax import lax
from jax.experimental import pallas as pl
from jax.experimental.pallas import tpu as pltpu

F32 = jnp.float32
BF16 = jnp.bfloat16

HEADS = 4
QK_CONV = 4
FFN_CONV = 3
G_RANK = 16
G_TAU_INV = 1.0 / 16.0
EPS = 1e-6
SMALL_W = 128
HALO = 8
NEG_INF = float("-inf")
FFN_TN = 512
FFN_TK = 1024

VMEM_LIMIT_V7X = 56 * 1024 * 1024


def _cparams(sem, vmem=VMEM_LIMIT_V7X):
    return pltpu.CompilerParams(dimension_semantics=sem, vmem_limit_bytes=vmem)


def _pick(n, prefs):
    for p in prefs:
        if n % p == 0:
            return p
    raise ValueError(f"no tile in {prefs} divides {n}")


def _log_sigmoid(x):
    return jnp.minimum(x, 0.0) - jnp.log1p(jnp.exp(-jnp.abs(x)))


def _dot(a, b):
    return jnp.dot(a, b, preferred_element_type=F32)


def _dot_hi(a, b):
    return jnp.dot(a, b, preferred_element_type=F32, precision=lax.Precision.HIGHEST)


def _dot_nt(a, b):
    return lax.dot_general(a, b, (((1,), (1,)), ((), ())), preferred_element_type=F32)


def _dot_tn(a, b):
    return lax.dot_general(a, b, (((0,), (0,)), ((), ())), preferred_element_type=F32)


def _rms(x):
    return x * lax.rsqrt(jnp.mean(x * x, axis=-1, keepdims=True) + EPS)


def _rmsnorm_kernel(x_ref, g_ref, o_ref):
    o_ref[...] = (_rms(x_ref[...].astype(F32)) * g_ref[...]).astype(o_ref.dtype)


def _rmsnorm(x, g, out_dtype):
    T, D = x.shape
    tr = _pick(T, (256, 128, 64, 8))
    return pl.pallas_call(
        _rmsnorm_kernel,
        grid=(T // tr,),
        in_specs=[pl.BlockSpec((tr, D), lambda i: (i, 0)),
                  pl.BlockSpec((1, D), lambda i: (0, 0))],
        out_specs=pl.BlockSpec((tr, D), lambda i: (i, 0)),
        out_shape=jax.ShapeDtypeStruct((T, D), out_dtype),
        compiler_params=_cparams(("parallel",)),
        name="rmsnorm",
    )(x, g.reshape(1, D).astype(F32))


def _matmul_kernel(a_ref, w_ref, o_ref):
    o_ref[...] = _dot(a_ref[...], w_ref[...]).astype(o_ref.dtype)


def _matmul(a, w, out_dtype, name):
    M, K = a.shape
    N = w.shape[1]
    tm = _pick(M, (1024, 512, 256, 128))
    tn = _pick(N, (1024, 512, 256, 128))
    return pl.pallas_call(
        _matmul_kernel,
        grid=(M // tm, N // tn),
        in_specs=[pl.BlockSpec((tm, K), lambda i, j: (i, 0)),
                  pl.BlockSpec((K, tn), lambda i, j: (0, j))],
        out_specs=pl.BlockSpec((tm, tn), lambda i, j: (i, j)),
        out_shape=jax.ShapeDtypeStruct((M, N), out_dtype),
        compiler_params=_cparams(("parallel", "arbitrary")),
        name=name,
    )(a, w)


def _outproj_kernel(ym_ref, yg_ref, wt_ref, wb_ref, x_ref, o_ref):
    acc = _dot(ym_ref[...], wt_ref[...]) + _dot(yg_ref[...], wb_ref[...])
    o_ref[...] = x_ref[...] + acc


def _outproj(ym, yg, w_top, w_bot, x):
    M, K2 = ym.shape
    N = w_top.shape[1]
    tm = _pick(M, (1024, 512, 256, 128))
    tn = _pick(N, (1024, 512, 256, 128))
    return pl.pallas_call(
        _outproj_kernel,
        grid=(M // tm, N // tn),
        in_specs=[pl.BlockSpec((tm, K2), lambda i, j: (i, 0)),
                  pl.BlockSpec((tm, K2), lambda i, j: (i, 0)),
                  pl.BlockSpec((K2, tn), lambda i, j: (0, j)),
                  pl.BlockSpec((K2, tn), lambda i, j: (0, j)),
                  pl.BlockSpec((tm, tn), lambda i, j: (i, j))],
        out_specs=pl.BlockSpec((tm, tn), lambda i, j: (i, j)),
        out_shape=jax.ShapeDtypeStruct((M, N), F32),
        compiler_params=_cparams(("parallel", "arbitrary")),
        name="outproj",
    )(ym, yg, w_top, w_bot, x)


def _conv_silu(raw_ref, x_ref, w_ref, b_ref, L):
    x_ref[HALO:HALO + L, :] = raw_ref[...].astype(F32)
    acc = b_ref[...]
    for j in range(QK_CONV):
        off = HALO - (QK_CONV - 1) + j
        acc = acc + w_ref[j:j + 1, :] * x_ref[off:off + L, :]
    x_ref[0:HALO, :] = x_ref[L:L + HALO, :]
    return acc * jax.nn.sigmoid(acc)


def _row_to_col(row, eye):
    L = eye.shape[0]
    return jnp.sum(jnp.where(eye, jnp.broadcast_to(row, (L, L)), 0.0), axis=1, keepdims=True)


def _mlstm_kernel(bias_ref, q_ref, k_ref, v_ref, o_ref, g_ref, cwq_ref, cwk_ref, cbq_ref,
                  cbk_ref, ng_ref, out_ref, C_ref, n_ref, m_ref, xq_ref, xk_ref, *, L, DK, DV):
    H = HEADS
    c = pl.program_id(1)

    @pl.when(c == 0)
    def _():
        C_ref[...] = jnp.zeros_like(C_ref)
        n_ref[...] = jnp.zeros_like(n_ref)
        m_ref[...] = jnp.zeros_like(m_ref)
        xq_ref[0:HALO, :] = jnp.zeros((HALO, H * DK), F32)
        xk_ref[0:HALO, :] = jnp.zeros((HALO, H * DK), F32)

    q_all = _conv_silu(q_ref, xq_ref, cwq_ref, cbq_ref, L) * (DK ** -0.5)
    k_all = _conv_silu(k_ref, xk_ref, cwk_ref, cbk_ref, L)

    G = g_ref[...] + bias_ref[...]
    bc = _log_sigmoid(G)
    lane = lax.broadcasted_iota(jnp.int32, (2 * H, L), 1)
    s = 1
    while s < L:
        bc = bc + jnp.where(lane >= s, pltpu.roll(bc, s, axis=1), 0.0)
        s *= 2

    row = lax.broadcasted_iota(jnp.int32, (L, L), 0)
    col = lax.broadcasted_iota(jnp.int32, (L, L), 1)
    eye = row == col
    causal = col <= row

    for h in range(H):
        q = q_all[:, h * DK:(h + 1) * DK]
        k = k_all[:, h * DK:(h + 1) * DK]
        v = v_ref[:, h * DV:(h + 1) * DV]
        gi = G[h:h + 1, :]
        b_row = bc[H + h:H + h + 1, :]
        b_col = _row_to_col(b_row, eye)

        m_prev = m_ref[h]
        C = C_ref[h]
        n = n_ref[h]
        D = jnp.where(causal, b_col - b_row + gi, NEG_INF)
        inter = b_col + m_prev
        m_t = jnp.maximum(inter, jnp.max(D, axis=1, keepdims=True))
        g = jnp.exp(inter - m_t)
        qb = q.astype(BF16)
        kb = k.astype(BF16)
        s_qk = _dot_nt(qb, kb) * jnp.exp(D - m_t)
        num = g * _dot(qb, C.astype(BF16)) + _dot(s_qk.astype(BF16), v)
        den = (g * jnp.sum(q * n, axis=1, keepdims=True)
               + jnp.sum(s_qk, axis=1, keepdims=True))
        hh = num / jnp.maximum(jnp.abs(den), jnp.exp(-m_t))

        y = (_rms(hh) * ng_ref[:, h * DV:(h + 1) * DV]
             * jax.nn.sigmoid(o_ref[:, h * DV:(h + 1) * DV].astype(F32)))
        out_ref[:, h * DV:(h + 1) * DV] = y.astype(out_ref.dtype)

        bL = b_row[:, L - 1:L]
        a_row = bL - b_row + gi
        m_new = jnp.maximum(bL + m_prev, jnp.max(a_row, axis=1, keepdims=True))
        ws_col = _row_to_col(jnp.exp(a_row - m_new), eye)
        gC = jnp.exp(bL + m_prev - m_new)
        kw = k * ws_col
        C_ref[h] = gC * C + _dot_tn(kw.astype(BF16), v)
        n_ref[h] = gC * n + jnp.sum(kw, axis=0, keepdims=True)
        m_ref[h] = m_new


def _mlstm(proj, gates, bias, conv_w, conv_b, norm_g, *, B, S, DK, DV, L):
    H = HEADS
    NC = S // L
    T = B * S
    kern = functools.partial(_mlstm_kernel, L=L, DK=DK, DV=DV)
    rowblk = lambda b, c: b * NC + c
    return pl.pallas_call(
        kern,
        grid=(B, NC),
        in_specs=[
            pl.BlockSpec((2 * H, 1), lambda b, c: (0, 0)),
            pl.BlockSpec((L, H * DK), lambda b, c: (rowblk(b, c), 0)),
            pl.BlockSpec((L, H * DK), lambda b, c: (rowblk(b, c), 1)),
            pl.BlockSpec((L, H * DV), lambda b, c: (rowblk(b, c), 1)),
            pl.BlockSpec((L, H * DV), lambda b, c: (rowblk(b, c), 2)),
            pl.BlockSpec((None, None, 2 * H, L), lambda b, c: (b, c, 0, 0)),
            pl.BlockSpec((QK_CONV, H * DK), lambda b, c: (0, 0)),
            pl.BlockSpec((QK_CONV, H * DK), lambda b, c: (0, 1)),
            pl.BlockSpec((1, H * DK), lambda b, c: (0, 0)),
            pl.BlockSpec((1, H * DK), lambda b, c: (0, 1)),
            pl.BlockSpec((1, H * DV), lambda b, c: (0, 0)),
        ],
        out_specs=pl.BlockSpec((L, H * DV), lambda b, c: (rowblk(b, c), 0)),
        out_shape=jax.ShapeDtypeStruct((T, H * DV), BF16),
        scratch_shapes=[
            pltpu.VMEM((H, DK, DV), F32),
            pltpu.VMEM((H, 1, DK), F32),
            pltpu.VMEM((H, 1, 1), F32),
            pltpu.VMEM((L + HALO, H * DK), F32),
            pltpu.VMEM((L + HALO, H * DK), F32),
        ],
        compiler_params=_cparams(("parallel", "arbitrary")),
        name="mlstm_scan",
    )(bias, proj, proj, proj, proj, gates, conv_w, conv_w, conv_b, conv_b, norm_g)


def _gla_kernel(q_ref, k_ref, v_ref, gg_ref, sm_ref, w2_ref, ab_ref, ng_ref, out_ref,
                ST_ref, bs_ref, *, L, DK, DV):
    H = HEADS
    c = pl.program_id(1)

    @pl.when(c == 0)
    def _():
        ST_ref[...] = jnp.zeros_like(ST_ref)
        bs_ref[...] = jnp.zeros_like(bs_ref)

    z = _dot_hi(sm_ref[...], w2_ref[...]) + ab_ref[...]
    la = _log_sigmoid(z) * G_TAU_INV
    row = lax.broadcasted_iota(jnp.int32, (L, L), 0)
    col = lax.broadcasted_iota(jnp.int32, (L, L), 1)
    tril = jnp.where(col <= row, 1.0, 0.0).astype(F32)
    b_all = _dot_hi(tril, la)
    bs_ref[HALO:HALO + L, :] = b_all
    pos_col = lax.broadcasted_iota(jnp.int32, (L, DK), 0)
    eye = row == col

    for h in range(H):
        dk = slice(h * DK, (h + 1) * DK)
        dv = slice(h * DV, (h + 1) * DV)
        b = b_all[:, dk]
        q = q_ref[:, dk].astype(F32) * (DK ** -0.5)
        k = k_ref[:, dk].astype(F32)
        v = v_ref[:, dv]
        ST = ST_ref[h]
        o = _dot_nt((q * jnp.exp(b)).astype(BF16), ST.astype(BF16))

        A = jnp.where(eye, _dot_nt(q.astype(BF16), k.astype(BF16)), 0.0)
        n = L
        while n >= 2:
            hn = n // 2
            pos = pos_col % n
            if n >= 8:
                R = jnp.concatenate(
                    [jnp.broadcast_to(b[i * n + hn - 1:i * n + hn, :], (n, DK))
                     for i in range(L // n)], axis=0)
            elif n == 4:
                R = jnp.where(pos == 0, bs_ref[HALO + 1:HALO + 1 + L, dk],
                              jnp.where(pos == 1, b,
                                        jnp.where(pos == 2, bs_ref[HALO - 1:HALO - 1 + L, dk],
                                                  bs_ref[HALO - 2:HALO - 2 + L, dk])))
            else:
                R = jnp.where(pos == 0, b, bs_ref[HALO - 1:HALO - 1 + L, dk])
            qn = q * jnp.exp(jnp.where(pos >= hn, b - R, NEG_INF))
            kn = k * jnp.exp(jnp.where(pos < hn, R - b, NEG_INF))
            S = _dot_nt(qn.astype(BF16), kn.astype(BF16))
            A = A + jnp.where((row // n) == (col // n), S, 0.0)
            n = hn
        o = o + _dot(A.astype(BF16), v)

        gg = gg_ref[:, dv].astype(F32)
        y = _rms(o) * ng_ref[:, dv] * (gg * jax.nn.sigmoid(gg))
        out_ref[:, dv] = y.astype(out_ref.dtype)

        bL = b[L - 1:L, :]
        kdec = k * jnp.exp(bL - b)
        ST_ref[h] = ST * jnp.exp(bL) + _dot_tn(v, kdec.astype(BF16))


def _gla(proj, small, w2p, a_b, norm_g, *, B, S, DK, DV, L, col0_k, col0_v):
    H = HEADS
    NC = S // L
    T = B * S
    kern = functools.partial(_gla_kernel, L=L, DK=DK, DV=DV)
    rowblk = lambda b, c: b * NC + c
    return pl.pallas_call(
        kern,
        grid=(B, NC),
        in_specs=[
            pl.BlockSpec((L, H * DK), lambda b, c: (rowblk(b, c), col0_k)),
            pl.BlockSpec((L, H * DK), lambda b, c: (rowblk(b, c), col0_k + 1)),
            pl.BlockSpec((L, H * DV), lambda b, c: (rowblk(b, c), col0_v)),
            pl.BlockSpec((L, H * DV), lambda b, c: (rowblk(b, c), col0_v + 1)),
            pl.BlockSpec((L, SMALL_W), lambda b, c: (rowblk(b, c), 0)),
            pl.BlockSpec((SMALL_W, H * DK), lambda b, c: (0, 0)),
            pl.BlockSpec((1, H * DK), lambda b, c: (0, 0)),
            pl.BlockSpec((1, H * DV), lambda b, c: (0, 0)),
        ],
        out_specs=pl.BlockSpec((L, H * DV), lambda b, c: (rowblk(b, c), 0)),
        out_shape=jax.ShapeDtypeStruct((T, H * DV), BF16),
        scratch_shapes=[
            pltpu.VMEM((H, DV, DK), F32),
            pltpu.VMEM((L + 2 * HALO, H * DK), F32),
        ],
        compiler_params=_cparams(("parallel", "arbitrary")),
        name="gla_scan",
    )(proj, proj, proj, proj, small, w2p, a_b, norm_g)


def _ffn_up_kernel(h_ref, halo_ref, wgu_ref, cw_ref, cb_ref, act_ref, gbuf_ref, *, tm, tn, S):
    i = pl.program_id(0)
    res = _dot(h_ref[...], wgu_ref[...])
    gate = res[:, :tn]
    up = res[:, tn:]
    seq_start = (i * tm) % S == 0
    ghalo = _dot(halo_ref[...], wgu_ref[:, :tn])
    gbuf_ref[0:HALO, :] = jnp.where(seq_start, 0.0, ghalo)
    gbuf_ref[HALO:HALO + tm, :] = gate
    conv = cb_ref[...] + cw_ref[FFN_CONV - 1:FFN_CONV, :] * gate
    for j in range(FFN_CONV - 1):
        off = HALO - (FFN_CONV - 1) + j
        conv = conv + cw_ref[j:j + 1, :] * gbuf_ref[off:off + tm, :]
    act_ref[...] = ((conv * jax.nn.sigmoid(conv)) * up).astype(act_ref.dtype)


def _ffn_up(h2, wgu, conv_w, conv_b, *, S, tn):
    T, D = h2.shape
    Fp = wgu.shape[1] // 2
    tm = _pick(S, (1024, 512, 256, 128))
    kern = functools.partial(_ffn_up_kernel, tm=tm, tn=tn, S=S)
    hb = tm // HALO
    return pl.pallas_call(
        kern,
        grid=(T // tm, Fp // tn),
        in_specs=[
            pl.BlockSpec((tm, D), lambda i, j: (i, 0)),
            pl.BlockSpec((HALO, D), lambda i, j: (jnp.maximum(i * hb - 1, 0), 0)),
            pl.BlockSpec((D, 2 * tn), lambda i, j: (0, j)),
            pl.BlockSpec((FFN_CONV, tn), lambda i, j: (0, j)),
            pl.BlockSpec((1, tn), lambda i, j: (0, j)),
        ],
        out_specs=pl.BlockSpec((tm, tn), lambda i, j: (i, j)),
        out_shape=jax.ShapeDtypeStruct((T, Fp), BF16),
        scratch_shapes=[pltpu.VMEM((tm + HALO, tn), F32)],
        compiler_params=_cparams(("parallel", "arbitrary")),
        name="ffn_up",
    )(h2, h2, wgu, conv_w, conv_b)


def _ffn_down_kernel(a_ref, wd_ref, x_ref, gf_ref, out_ref, *, tn):
    k = pl.program_id(1)
    nk = pl.num_programs(1)
    D = out_ref.shape[1]

    @pl.when(k == 0)
    def _():
        out_ref[...] = x_ref[...]

    a = a_ref[...]
    for n0 in range(0, D, tn):
        out_ref[:, n0:n0 + tn] += _dot(a, wd_ref[:, n0:n0 + tn])

    @pl.when(k == nk - 1)
    def _():
        out_ref[...] = _rms(out_ref[...]) * gf_ref[...]


def _ffn_down(act, wd, x1, lnf_g, *, tk):
    T, Fp = act.shape
    D = wd.shape[1]
    tm = _pick(T, (512, 256, 128))
    tn = _pick(D, (1024, 512, 256, 128))
    kern = functools.partial(_ffn_down_kernel, tn=tn)
    return pl.pallas_call(
        kern,
        grid=(T // tm, Fp // tk),
        in_specs=[
            pl.BlockSpec((tm, tk), lambda i, k: (i, k)),
            pl.BlockSpec((tk, D), lambda i, k: (k, 0)),
            pl.BlockSpec((tm, D), lambda i, k: (i, 0), pipeline_mode=pl.Buffered(1)),
            pl.BlockSpec((1, D), lambda i, k: (0, 0)),
        ],
        out_specs=pl.BlockSpec((tm, D), lambda i, k: (i, 0)),
        out_shape=jax.ShapeDtypeStruct((T, D), F32),
        compiler_params=_cparams(("parallel", "arbitrary")),
        name="ffn_down",
    )(act, wd, x1, lnf_g)


def kernel(x, ln1_g, w_in, mlstm_conv_w, mlstm_conv_b, mlstm_i_b, mlstm_f_b, mlstm_norm_g,
           gla_a_w2, gla_a_b, gla_norm_g, w_out, ln2_g, w_ffn_gate, w_ffn_up,
           ffn_conv_w, ffn_conv_b, w_ffn_down, lnf_g):
    B, S, D = x.shape
    T = B * S
    H = HEADS
    M_DK = mlstm_conv_w.shape[1] // (2 * H)
    M_DV = mlstm_norm_g.shape[0] // H
    G_DK = gla_a_w2.shape[1] // H
    G_DV = gla_norm_g.shape[0] // H
    F = w_ffn_gate.shape[1]
    assert M_DV == 2 * M_DK and G_DV == 2 * G_DK and M_DK == G_DK

    n_m = 2 * H * M_DK + 2 * H * M_DV
    n_g = 2 * H * G_DK + 2 * H * G_DV
    o_gate = n_m
    o_g = n_m + 2 * H
    o_a = o_g + n_g
    assert w_in.shape[1] == o_a + G_RANK

    w_big = jnp.concatenate([w_in[:, :n_m], w_in[:, o_g:o_a]], axis=1).astype(BF16)
    w_small = jnp.concatenate(
        [w_in[:, o_gate:o_g], w_in[:, o_a:],
         jnp.zeros((D, SMALL_W - 2 * H - G_RANK), w_in.dtype)], axis=1).astype(BF16)

    x2d = x.reshape(T, D)
    h1 = _rmsnorm(x2d, ln1_g, BF16)
    proj = _matmul(h1, w_big, BF16, "inproj")
    small = _matmul(h1, w_small, F32, "inproj_small")

    L_M = 128
    gates = small[:, :2 * H].reshape(B, S // L_M, L_M, 2 * H)
    gates = jnp.transpose(gates, (0, 1, 3, 2))
    bias = jnp.concatenate([mlstm_i_b, mlstm_f_b]).astype(F32).reshape(2 * H, 1)
    ym = _mlstm(proj, gates, bias, mlstm_conv_w.astype(F32),
                mlstm_conv_b.reshape(1, -1).astype(F32), mlstm_norm_g.reshape(1, -1).astype(F32),
                B=B, S=S, DK=M_DK, DV=M_DV, L=L_M)

    L_G = 64
    w2p = jnp.zeros((SMALL_W, H * G_DK), F32).at[2 * H:2 * H + G_RANK].set(gla_a_w2.astype(F32))
    yg = _gla(proj, small, w2p, gla_a_b.reshape(1, -1).astype(F32),
              gla_norm_g.reshape(1, -1).astype(F32),
              B=B, S=S, DK=G_DK, DV=G_DV, L=L_G,
              col0_k=n_m // (H * G_DK), col0_v=(n_m + 2 * H * G_DK) // (H * G_DV))

    w_out_b = w_out.astype(BF16)
    x1 = _outproj(ym, yg, w_out_b[:H * M_DV], w_out_b[H * M_DV:], x2d)

    h2 = _rmsnorm(x1, ln2_g, BF16)
    Fp = -(-F // FFN_TK) * FFN_TK
    tn = FFN_TN
    padc = lambda w: jnp.pad(w, ((0, 0), (0, Fp - F)))
    wgu = jnp.stack([padc(w_ffn_gate).reshape(D, Fp // tn, tn),
                     padc(w_ffn_up).reshape(D, Fp // tn, tn)], axis=2)
    wgu = wgu.reshape(D, 2 * Fp).astype(BF16)
    act = _ffn_up(h2, wgu, padc(ffn_conv_w).astype(F32),
                  padc(ffn_conv_b.reshape(1, -1)).astype(F32), S=S, tn=tn)
    wd = jnp.pad(w_ffn_down, ((0, Fp - F), (0, 0))).astype(BF16)
    out = _ffn_down(act, wd, x1, lnf_g.reshape(1, -1).astype(F32), tk=FFN_TK)
    return out.reshape(B, S, D)
```

```python
import functools

import jax
import jax.numpy as jnp
from jax import lax
from jax.experimental import pallas as pl
from jax.experimental.pallas import tpu as pltpu

F32 = jnp.float32
BF16 = jnp.bfloat16

HEADS = 4
QK_CONV = 4
FFN_CONV = 3
G_RANK = 16
G_TAU_INV = 1.0 / 16.0
EPS = 1e-6
SMALL_W = 128
HALO = 8
NEG_INF = float("-inf")
FFN_TN = 512
FFN_TK = 1024
MXU_COLS_V7X = 256

VMEM_LIMIT_V7X = 56 * 1024 * 1024


def _cparams(sem, vmem=VMEM_LIMIT_V7X):
    return pltpu.CompilerParams(dimension_semantics=sem, vmem_limit_bytes=vmem)


def _pick(n, prefs):
    for p in prefs:
        if n % p == 0:
            return p
    raise ValueError(f"no tile in {prefs} divides {n}")


def _log_sigmoid(x):
    return jnp.minimum(x, 0.0) - jnp.log1p(jnp.exp(-jnp.abs(x)))


def _dot(a, b):
    return jnp.dot(a, b, preferred_element_type=F32)


def _dot_hi(a, b):
    return jnp.dot(a, b, preferred_element_type=F32, precision=lax.Precision.HIGHEST)


def _dot_nt(a, b):
    return lax.dot_general(a, b, (((1,), (1,)), ((), ())), preferred_element_type=F32)


def _dot_tn(a, b):
    return lax.dot_general(a, b, (((0,), (0,)), ((), ())), preferred_element_type=F32)


def _rms(x):
    return x * lax.rsqrt(jnp.mean(x * x, axis=-1, keepdims=True) + EPS)


def _rmsnorm_kernel(x_ref, g_ref, o_ref):
    o_ref[...] = (_rms(x_ref[...].astype(F32)) * g_ref[...]).astype(o_ref.dtype)


def _rmsnorm(x, g, out_dtype):
    T, D = x.shape
    tr = _pick(T, (256, 128, 64, 8))
    return pl.pallas_call(
        _rmsnorm_kernel,
        grid=(T // tr,),
        in_specs=[pl.BlockSpec((tr, D), lambda i: (i, 0)),
                  pl.BlockSpec((1, D), lambda i: (0, 0))],
        out_specs=pl.BlockSpec((tr, D), lambda i: (i, 0)),
        out_shape=jax.ShapeDtypeStruct((T, D), out_dtype),
        compiler_params=_cparams(("parallel",)),
        name="rmsnorm",
    )(x, g.reshape(1, D).astype(F32))


def _matmul_nt_kernel(a_ref, wt_ref, o_ref):
    o_ref[...] = _dot_nt(a_ref[...], wt_ref[...]).astype(o_ref.dtype)


def _matmul_nt(a, wt, out_dtype, name, n_out=None):
    M, K = a.shape
    N = wt.shape[0] if n_out is None else n_out
    tm = _pick(M, (1024, 512, 256, 128))
    tn = _pick(N, (1024, 512, 256, 128))
    return pl.pallas_call(
        _matmul_nt_kernel,
        grid=(M // tm, N // tn),
        in_specs=[pl.BlockSpec((tm, K), lambda i, j: (i, 0)),
                  pl.BlockSpec((tn, K), lambda i, j: (j, 0))],
        out_specs=pl.BlockSpec((tm, tn), lambda i, j: (i, j)),
        out_shape=jax.ShapeDtypeStruct((M, N), out_dtype),
        compiler_params=_cparams(("parallel", "arbitrary")),
        name=name,
    )(a, wt)


def _outproj_kernel(ym_ref, yg_ref, wt_ref, wb_ref, x_ref, o_ref):
    acc = _dot(ym_ref[...], wt_ref[...]) + _dot(yg_ref[...], wb_ref[...])
    o_ref[...] = x_ref[...] + acc


def _outproj(ym, yg, w, x):
    M, K2 = ym.shape
    N = w.shape[1]
    tm = _pick(M, (1024, 512, 256, 128))
    tn = _pick(N, (1024, 512, 256, 128))
    return pl.pallas_call(
        _outproj_kernel,
        grid=(M // tm, N // tn),
        in_specs=[pl.BlockSpec((tm, K2), lambda i, j: (i, 0)),
                  pl.BlockSpec((tm, K2), lambda i, j: (i, 0)),
                  pl.BlockSpec((K2, tn), lambda i, j: (0, j)),
                  pl.BlockSpec((K2, tn), lambda i, j: (1, j)),
                  pl.BlockSpec((tm, tn), lambda i, j: (i, j))],
        out_specs=pl.BlockSpec((tm, tn), lambda i, j: (i, j)),
        out_shape=jax.ShapeDtypeStruct((M, N), F32),
        compiler_params=_cparams(("parallel", "arbitrary")),
        name="outproj",
    )(ym, yg, w, w, x)


def _conv_silu(raw_ref, x_ref, w_ref, b_ref, L):
    x_ref[HALO:HALO + L, :] = raw_ref[...].astype(F32)
    acc = b_ref[...]
    for j in range(QK_CONV):
        off = HALO - (QK_CONV - 1) + j
        acc = acc + w_ref[j:j + 1, :] * x_ref[off:off + L, :]
    x_ref[0:HALO, :] = x_ref[L:L + HALO, :]
    return acc * jax.nn.sigmoid(acc)


def _row_to_col(row, eye):
    L = eye.shape[0]
    return jnp.sum(jnp.where(eye, jnp.broadcast_to(row, (L, L)), 0.0), axis=1, keepdims=True)


def _mlstm_kernel(bias_ref, q_ref, k_ref, v_ref, o_ref, g_ref, cwq_ref, cwk_ref, cbq_ref,
                  cbk_ref, ng_ref, out_ref, C_ref, n_ref, m_ref, xq_ref, xk_ref, *, L, DK, DV):
    H = HEADS
    c = pl.program_id(1)

    @pl.when(c == 0)
    def _():
        C_ref[...] = jnp.zeros_like(C_ref)
        n_ref[...] = jnp.zeros_like(n_ref)
        m_ref[...] = jnp.zeros_like(m_ref)
        xq_ref[0:HALO, :] = jnp.zeros((HALO, H * DK), F32)
        xk_ref[0:HALO, :] = jnp.zeros((HALO, H * DK), F32)

    q_all = _conv_silu(q_ref, xq_ref, cwq_ref, cbq_ref, L) * (DK ** -0.5)
    k_all = _conv_silu(k_ref, xk_ref, cwk_ref, cbk_ref, L)

    G = g_ref[...] + bias_ref[...]
    bc = _log_sigmoid(G)
    lane = lax.broadcasted_iota(jnp.int32, (2 * H, L), 1)
    s = 1
    while s < L:
        bc = bc + jnp.where(lane >= s, pltpu.roll(bc, s, axis=1), 0.0)
        s *= 2

    row = lax.broadcasted_iota(jnp.int32, (L, L), 0)
    col = lax.broadcasted_iota(jnp.int32, (L, L), 1)
    eye = row == col
    causal = col <= row

    for h in range(H):
        q = q_all[:, h * DK:(h + 1) * DK]
        k = k_all[:, h * DK:(h + 1) * DK]
        v = v_ref[:, h * DV:(h + 1) * DV]
        gi = G[h:h + 1, :]
        b_row = bc[H + h:H + h + 1, :]
        b_col = _row_to_col(b_row, eye)

        m_prev = m_ref[h]
        C = C_ref[h]
        n = n_ref[h]
        D = jnp.where(causal, b_col - b_row + gi, NEG_INF)
        inter = b_col + m_prev
        m_t = jnp.maximum(inter, jnp.max(D, axis=1, keepdims=True))
        g = jnp.exp(inter - m_t)
        qb = q.astype(BF16)
        kb = k.astype(BF16)
        s_qk = _dot_nt(qb, kb) * jnp.exp(D - m_t)
        num = g * _dot(qb, C.astype(BF16)) + _dot(s_qk.astype(BF16), v)
        den = (g * jnp.sum(q * n, axis=1, keepdims=True)
               + jnp.sum(s_qk, axis=1, keepdims=True))
        hh = num / jnp.maximum(jnp.abs(den), jnp.exp(-m_t))

        y = (_rms(hh) * ng_ref[:, h * DV:(h + 1) * DV]
             * jax.nn.sigmoid(o_ref[:, h * DV:(h + 1) * DV].astype(F32)))
        out_ref[:, h * DV:(h + 1) * DV] = y.astype(out_ref.dtype)

        bL = b_row[:, L - 1:L]
        a_row = bL - b_row + gi
        m_new = jnp.maximum(bL + m_prev, jnp.max(a_row, axis=1, keepdims=True))
        ws_col = _row_to_col(jnp.exp(a_row - m_new), eye)
        gC = jnp.exp(bL + m_prev - m_new)
        kw = k * ws_col
        C_ref[h] = gC * C + _dot_tn(kw.astype(BF16), v)
        n_ref[h] = gC * n + jnp.sum(kw, axis=0, keepdims=True)
        m_ref[h] = m_new


def _mlstm(proj, gates, bias, conv_w, conv_b, norm_g, *, B, S, DK, DV, L):
    H = HEADS
    NC = S // L
    T = B * S
    kern = functools.partial(_mlstm_kernel, L=L, DK=DK, DV=DV)
    rowblk = lambda b, c: b * NC + c
    return pl.pallas_call(
        kern,
        grid=(B, NC),
        in_specs=[
            pl.BlockSpec((2 * H, 1), lambda b, c: (0, 0)),
            pl.BlockSpec((L, H * DK), lambda b, c: (rowblk(b, c), 0)),
            pl.BlockSpec((L, H * DK), lambda b, c: (rowblk(b, c), 1)),
            pl.BlockSpec((L, H * DV), lambda b, c: (rowblk(b, c), 1)),
            pl.BlockSpec((L, H * DV), lambda b, c: (rowblk(b, c), 2)),
            pl.BlockSpec((None, None, 2 * H, L), lambda b, c: (b, c, 0, 0)),
            pl.BlockSpec((QK_CONV, H * DK), lambda b, c: (0, 0)),
            pl.BlockSpec((QK_CONV, H * DK), lambda b, c: (0, 1)),
            pl.BlockSpec((1, H * DK), lambda b, c: (0, 0)),
            pl.BlockSpec((1, H * DK), lambda b, c: (0, 1)),
            pl.BlockSpec((1, H * DV), lambda b, c: (0, 0)),
        ],
        out_specs=pl.BlockSpec((L, H * DV), lambda b, c: (rowblk(b, c), 0)),
        out_shape=jax.ShapeDtypeStruct((T, H * DV), BF16),
        scratch_shapes=[
            pltpu.VMEM((H, DK, DV), F32),
            pltpu.VMEM((H, 1, DK), F32),
            pltpu.VMEM((H, 1, 1), F32),
            pltpu.VMEM((L + HALO, H * DK), F32),
            pltpu.VMEM((L + HALO, H * DK), F32),
        ],
        compiler_params=_cparams(("parallel", "arbitrary")),
        name="mlstm_scan",
    )(bias, proj, proj, proj, proj, gates, conv_w, conv_w, conv_b, conv_b, norm_g)


def _gla_kernel(q_ref, k_ref, v_ref, gg_ref, sm_ref, w2_ref, ab_ref, ng_ref, out_ref,
                ST_ref, bs_ref, *, L, DK, DV):
    H = HEADS
    c = pl.program_id(1)

    @pl.when(c == 0)
    def _():
        ST_ref[...] = jnp.zeros_like(ST_ref)
        bs_ref[...] = jnp.zeros_like(bs_ref)

    z = _dot_hi(sm_ref[...], w2_ref[...]) + ab_ref[...]
    la = _log_sigmoid(z) * G_TAU_INV
    row = lax.broadcasted_iota(jnp.int32, (L, L), 0)
    col = lax.broadcasted_iota(jnp.int32, (L, L), 1)
    tril = jnp.where(col <= row, 1.0, 0.0).astype(F32)
    b_all = _dot_hi(tril, la)
    bs_ref[HALO:HALO + L, :] = b_all
    pos_col = lax.broadcasted_iota(jnp.int32, (L, DK), 0)
    eye = row == col

    for h in range(H):
        dk = slice(h * DK, (h + 1) * DK)
        dv = slice(h * DV, (h + 1) * DV)
        b = b_all[:, dk]
        q = q_ref[:, dk].astype(F32) * (DK ** -0.5)
        k = k_ref[:, dk].astype(F32)
        v = v_ref[:, dv]
        ST = ST_ref[h]
        o = _dot_nt((q * jnp.exp(b)).astype(BF16), ST.astype(BF16))

        A = jnp.where(eye, _dot_nt(q.astype(BF16), k.astype(BF16)), 0.0)
        n = L
        while n >= 2:
            hn = n // 2
            pos = pos_col % n
            if n >= 8:
                R = jnp.concatenate(
                    [jnp.broadcast_to(b[i * n + hn - 1:i * n + hn, :], (n, DK))
                     for i in range(L // n)], axis=0)
            elif n == 4:
                R = jnp.where(pos == 0, bs_ref[HALO + 1:HALO + 1 + L, dk],
                              jnp.where(pos == 1, b,
                                        jnp.where(pos == 2, bs_ref[HALO - 1:HALO - 1 + L, dk],
                                                  bs_ref[HALO - 2:HALO - 2 + L, dk])))
            else:
                R = jnp.where(pos == 0, b, bs_ref[HALO - 1:HALO - 1 + L, dk])
            qn = q * jnp.exp(jnp.where(pos >= hn, b - R, NEG_INF))
            kn = k * jnp.exp(jnp.where(pos < hn, R - b, NEG_INF))
            S = _dot_nt(qn.astype(BF16), kn.astype(BF16))
            A = A + jnp.where((row // n) == (col // n), S, 0.0)
            n = hn
        o = o + _dot(A.astype(BF16), v)

        gg = gg_ref[:, dv].astype(F32)
        y = _rms(o) * ng_ref[:, dv] * (gg * jax.nn.sigmoid(gg))
        out_ref[:, dv] = y.astype(out_ref.dtype)

        bL = b[L - 1:L, :]
        kdec = k * jnp.exp(bL - b)
        ST_ref[h] = ST * jnp.exp(bL) + _dot_tn(v, kdec.astype(BF16))


def _gla(proj, small, w2p, a_b, norm_g, *, B, S, DK, DV, L):
    H = HEADS
    NC = S // L
    T = B * S
    kern = functools.partial(_gla_kernel, L=L, DK=DK, DV=DV)
    rowblk = lambda b, c: b * NC + c
    return pl.pallas_call(
        kern,
        grid=(B, NC),
        in_specs=[
            pl.BlockSpec((L, H * DK), lambda b, c: (rowblk(b, c), 0)),
            pl.BlockSpec((L, H * DK), lambda b, c: (rowblk(b, c), 1)),
            pl.BlockSpec((L, H * DV), lambda b, c: (rowblk(b, c), 1)),
            pl.BlockSpec((L, H * DV), lambda b, c: (rowblk(b, c), 2)),
            pl.BlockSpec((L, SMALL_W), lambda b, c: (rowblk(b, c), 0)),
            pl.BlockSpec((SMALL_W, H * DK), lambda b, c: (0, 0)),
            pl.BlockSpec((1, H * DK), lambda b, c: (0, 0)),
            pl.BlockSpec((1, H * DV), lambda b, c: (0, 0)),
        ],
        out_specs=pl.BlockSpec((L, H * DV), lambda b, c: (rowblk(b, c), 0)),
        out_shape=jax.ShapeDtypeStruct((T, H * DV), BF16),
        scratch_shapes=[
            pltpu.VMEM((H, DV, DK), F32),
            pltpu.VMEM((L + 2 * HALO, H * DK), F32),
        ],
        compiler_params=_cparams(("parallel", "arbitrary")),
        name="gla_scan",
    )(proj, proj, proj, proj, small, w2p, a_b, norm_g)


def _ffn_up_kernel(h_ref, wg_ref, wu_ref, cw_ref, cb_ref, act_ref, gbuf_ref, ghalo_ref,
                   *, tm, S, tg, groups_last):
    i = pl.program_id(0)
    j = pl.program_id(1)
    nj = pl.num_programs(1)
    ng = act_ref.shape[1] // tg
    seq_start = (i * tm) % S == 0

    @pl.when(jnp.logical_and(i == 0, j == 0))
    def _():
        ghalo_ref[...] = jnp.zeros_like(ghalo_ref)

    def group(g):
        cols = slice(g * tg, (g + 1) * tg)
        hb = h_ref[...]
        gate = _dot(hb, wg_ref[:, cols])
        up = _dot(hb, wu_ref[:, cols])
        gbuf_ref[g, 0:HALO, :] = jnp.where(seq_start, 0.0, ghalo_ref[j, :, cols])
        gbuf_ref[g, HALO:HALO + tm, :] = gate
        ghalo_ref[j, :, cols] = gate[tm - HALO:tm, :]
        conv = cb_ref[:, cols] + cw_ref[FFN_CONV - 1:FFN_CONV, cols] * gate
        for t in range(FFN_CONV - 1):
            off = HALO - (FFN_CONV - 1) + t
            conv = conv + cw_ref[t:t + 1, cols] * gbuf_ref[g, off:off + tm, :]
        act_ref[:, cols] = ((conv * jax.nn.sigmoid(conv)) * up).astype(act_ref.dtype)

    def body(n_groups):
        for g in range(n_groups):
            group(g)

    if groups_last == ng:
        body(ng)
    else:
        pl.when(j < nj - 1)(functools.partial(body, ng))
        pl.when(j == nj - 1)(functools.partial(body, groups_last))


def _ffn_up(h2, wg, wu, conv_w, conv_b, *, S):
    T, D = h2.shape
    F = wg.shape[1]
    tm = _pick(S, (1024, 512, 256, 128))
    tg = _pick(F, (MXU_COLS_V7X, 128))
    tn = FFN_TN
    nj = -(-F // tn)
    groups_last = (F - (nj - 1) * tn) // tg
    kern = functools.partial(_ffn_up_kernel, tm=tm, S=S, tg=tg, groups_last=groups_last)
    return pl.pallas_call(
        kern,
        grid=(T // tm, nj),
        in_specs=[
            pl.BlockSpec((tm, D), lambda i, j: (i, 0)),
            pl.BlockSpec((D, tn), lambda i, j: (0, j)),
            pl.BlockSpec((D, tn), lambda i, j: (0, j)),
            pl.BlockSpec((FFN_CONV, tn), lambda i, j: (0, j)),
            pl.BlockSpec((1, tn), lambda i, j: (0, j)),
        ],
        out_specs=pl.BlockSpec((tm, tn), lambda i, j: (i, j)),
        out_shape=jax.ShapeDtypeStruct((T, F), BF16),
        scratch_shapes=[pltpu.VMEM((tn // tg, tm + HALO, tg), F32),
                        pltpu.VMEM((nj, HALO, tn), F32)],
        compiler_params=_cparams(("arbitrary", "arbitrary")),
        name="ffn_up",
    )(h2, wg, wu, conv_w, conv_b)


def _ffn_down_kernel(a_ref, wd_ref, x_ref, gf_ref, out_ref, *, tn, k_tail):
    k = pl.program_id(1)
    nk = pl.num_programs(1)
    tk = a_ref.shape[1]
    D = out_ref.shape[1]

    @pl.when(k == 0)
    def _():
        out_ref[...] = x_ref[...]

    def accumulate(kk):
        a = a_ref[:, :kk]
        for n0 in range(0, D, tn):
            out_ref[:, n0:n0 + tn] += _dot(a, wd_ref[:kk, n0:n0 + tn])

    if k_tail == tk:
        accumulate(tk)
    else:
        pl.when(k < nk - 1)(lambda: accumulate(tk))
        pl.when(k == nk - 1)(lambda: accumulate(k_tail))

    @pl.when(k == nk - 1)
    def _():
        out_ref[...] = _rms(out_ref[...]) * gf_ref[...]


def _ffn_down(act, wd, x1, lnf_g):
    T, F = act.shape
    D = wd.shape[1]
    tm = _pick(T, (512, 256, 128))
    tn = _pick(D, (1024, 512, 256, 128))
    tk = FFN_TK
    nk = -(-F // tk)
    kern = functools.partial(_ffn_down_kernel, tn=tn, k_tail=F - (nk - 1) * tk)
    return pl.pallas_call(
        kern,
        grid=(T // tm, nk),
        in_specs=[
            pl.BlockSpec((tm, tk), lambda i, k: (i, k)),
            pl.BlockSpec((tk, D), lambda i, k: (k, 0)),
            pl.BlockSpec((tm, D), lambda i, k: (i, 0)),
            pl.BlockSpec((1, D), lambda i, k: (0, 0)),
        ],
        out_specs=pl.BlockSpec((tm, D), lambda i, k: (i, 0)),
        out_shape=jax.ShapeDtypeStruct((T, D), F32),
        compiler_params=_cparams(("parallel", "arbitrary")),
        name="ffn_down",
    )(act, wd, x1, lnf_g)


def kernel(x, ln1_g, w_in, mlstm_conv_w, mlstm_conv_b, mlstm_i_b, mlstm_f_b, mlstm_norm_g,
           gla_a_w2, gla_a_b, gla_norm_g, w_out, ln2_g, w_ffn_gate, w_ffn_up,
           ffn_conv_w, ffn_conv_b, w_ffn_down, lnf_g):
    B, S, D = x.shape
    T = B * S
    H = HEADS
    M_DK = mlstm_conv_w.shape[1] // (2 * H)
    M_DV = mlstm_norm_g.shape[0] // H
    G_DK = gla_a_w2.shape[1] // H
    G_DV = gla_norm_g.shape[0] // H
    F = w_ffn_gate.shape[1]
    assert M_DV == 2 * M_DK and G_DV == 2 * G_DK and M_DK == G_DK

    n_m = 2 * H * M_DK + 2 * H * M_DV
    n_g = 2 * H * G_DK + 2 * H * G_DV
    o_gate = n_m
    o_g = n_m + 2 * H
    o_a = o_g + n_g
    assert w_in.shape[1] == o_a + G_RANK

    w_in_t = w_in.T.astype(BF16)
    w_g = w_in_t[o_g:o_a]
    w_small = jnp.concatenate(
        [w_in_t[o_gate:o_g], w_in_t[o_a:],
         jnp.zeros((SMALL_W - 2 * H - G_RANK, D), BF16)], axis=0)

    x2d = x.reshape(T, D)
    h1 = _rmsnorm(x2d, ln1_g, BF16)
    proj_m = _matmul_nt(h1, w_in_t, BF16, "inproj_mlstm", n_out=n_m)
    proj_g = _matmul_nt(h1, w_g, BF16, "inproj_gla")
    small = _matmul_nt(h1, w_small, F32, "inproj_small")

    L_M = 128
    gates = small[:, :2 * H].reshape(B, S // L_M, L_M, 2 * H)
    gates = jnp.transpose(gates, (0, 1, 3, 2))
    bias = jnp.concatenate([mlstm_i_b, mlstm_f_b]).astype(F32).reshape(2 * H, 1)
    ym = _mlstm(proj_m, gates, bias, mlstm_conv_w.astype(F32),
                mlstm_conv_b.reshape(1, -1).astype(F32), mlstm_norm_g.reshape(1, -1).astype(F32),
                B=B, S=S, DK=M_DK, DV=M_DV, L=L_M)

    L_G = 64
    w2p = jnp.zeros((SMALL_W, H * G_DK), F32).at[2 * H:2 * H + G_RANK].set(gla_a_w2.astype(F32))
    yg = _gla(proj_g, small, w2p, gla_a_b.reshape(1, -1).astype(F32),
              gla_norm_g.reshape(1, -1).astype(F32), B=B, S=S, DK=G_DK, DV=G_DV, L=L_G)

    x1 = _outproj(ym, yg, w_out.astype(BF16), x2d)

    h2 = _rmsnorm(x1, ln2_g, BF16)
    act = _ffn_up(h2, w_ffn_gate.astype(BF16), w_ffn_up.astype(BF16), ffn_conv_w.astype(F32),
                  ffn_conv_b.reshape(1, -1).astype(F32), S=S)
    out = _ffn_down(act, w_ffn_down.astype(BF16), x1, lnf_g.reshape(1, -1).astype(F32))
    return out.reshape(B, S, D)
```

```python
import functools

import jax
import jax.numpy as jnp
from jax import lax
from jax.experimental import pallas as pl
from jax.experimental.pallas import tpu as pltpu

F32 = jnp.float32
BF16 = jnp.bfloat16

HEADS = 4
QK_CONV = 4
FFN_CONV = 3
G_RANK = 16
G_TAU_INV = 1.0 / 16.0
EPS = 1e-6
SMALL_W = 128
HALO = 8
NEG_INF = float("-inf")
FFN_TN = 512
FFN_TK = 1024
MXU_COLS_V7X = 256
NORM_ROWS = 128

VMEM_LIMIT_V7X = 56 * 1024 * 1024


def _cparams(sem, vmem=VMEM_LIMIT_V7X):
    return pltpu.CompilerParams(dimension_semantics=sem, vmem_limit_bytes=vmem)


def _pick(n, prefs):
    for p in prefs:
        if n % p == 0:
            return p
    raise ValueError(f"no tile in {prefs} divides {n}")


def _log_sigmoid(x):
    return jnp.minimum(x, 0.0) - jnp.log1p(jnp.exp(-jnp.abs(x)))


def _dot(a, b):
    return jnp.dot(a, b, preferred_element_type=F32)


def _dot_hi(a, b):
    return jnp.dot(a, b, preferred_element_type=F32, precision=lax.Precision.HIGHEST)


def _dot_nt(a, b):
    return lax.dot_general(a, b, (((1,), (1,)), ((), ())), preferred_element_type=F32)


def _dot_tn(a, b):
    return lax.dot_general(a, b, (((0,), (0,)), ((), ())), preferred_element_type=F32)


def _rms(x):
    return x * lax.rsqrt(jnp.mean(x * x, axis=-1, keepdims=True) + EPS)


def _rmsnorm_kernel(x_ref, g_ref, o_ref):
    o_ref[...] = (_rms(x_ref[...].astype(F32)) * g_ref[...]).astype(o_ref.dtype)


def _rmsnorm(x, g, out_dtype):
    T, D = x.shape
    tr = _pick(T, (256, 128, 64, 8))
    return pl.pallas_call(
        _rmsnorm_kernel,
        grid=(T // tr,),
        in_specs=[pl.BlockSpec((tr, D), lambda i: (i, 0)),
                  pl.BlockSpec((1, D), lambda i: (0, 0))],
        out_specs=pl.BlockSpec((tr, D), lambda i: (i, 0)),
        out_shape=jax.ShapeDtypeStruct((T, D), out_dtype),
        compiler_params=_cparams(("parallel",)),
        name="rmsnorm",
    )(x, g.reshape(1, D).astype(F32))


def _matmul_nt_kernel(a_ref, wt_ref, o_ref):
    o_ref[...] = _dot_nt(a_ref[...], wt_ref[...]).astype(o_ref.dtype)


def _matmul_nt(a, wt, out_dtype, name, n_out=None):
    M, K = a.shape
    N = wt.shape[0] if n_out is None else n_out
    tm = _pick(M, (1024, 512, 256, 128))
    tn = _pick(N, (1024, 512, 256, 128))
    return pl.pallas_call(
        _matmul_nt_kernel,
        grid=(M // tm, N // tn),
        in_specs=[pl.BlockSpec((tm, K), lambda i, j: (i, 0)),
                  pl.BlockSpec((tn, K), lambda i, j: (j, 0))],
        out_specs=pl.BlockSpec((tm, tn), lambda i, j: (i, j)),
        out_shape=jax.ShapeDtypeStruct((M, N), out_dtype),
        compiler_params=_cparams(("parallel", "arbitrary")),
        name=name,
    )(a, wt)


def _outproj_kernel(ym_ref, yg_ref, w_ref, x_ref, g_ref, x1_ref, h2_ref, *, tn):
    k = pl.program_id(1)
    nk = pl.num_programs(1)
    N = x1_ref.shape[1]

    @pl.when(k == 0)
    def _():
        x1_ref[...] = x_ref[...]

    a = jnp.where(k < nk // 2, ym_ref[...], yg_ref[...])
    for n0 in range(0, N, tn):
        x1_ref[:, n0:n0 + tn] += _dot(a, w_ref[:, n0:n0 + tn])

    @pl.when(k == nk - 1)
    def _():
        for r0 in range(0, x1_ref.shape[0], NORM_ROWS):
            rows = slice(r0, r0 + NORM_ROWS)
            h2_ref[rows, :] = (_rms(x1_ref[rows, :]) * g_ref[...]).astype(h2_ref.dtype)


def _outproj(ym, yg, w, x, g):
    M, K2 = ym.shape
    N = w.shape[1]
    tm = _pick(M, (512, 256, 128))
    tk = _pick(K2, (512, 256, 128))
    tn = _pick(N, (1024, 512, 256, 128))
    nh = K2 // tk
    kern = functools.partial(_outproj_kernel, tn=tn)
    return pl.pallas_call(
        kern,
        grid=(M // tm, 2 * nh),
        in_specs=[pl.BlockSpec((tm, tk), lambda i, k: (i, jnp.minimum(k, nh - 1))),
                  pl.BlockSpec((tm, tk), lambda i, k: (i, jnp.maximum(k - nh, 0))),
                  pl.BlockSpec((tk, N), lambda i, k: (k, 0)),
                  pl.BlockSpec((tm, N), lambda i, k: (i, 0)),
                  pl.BlockSpec((1, N), lambda i, k: (0, 0))],
        out_specs=[pl.BlockSpec((tm, N), lambda i, k: (i, 0)),
                   pl.BlockSpec((tm, N), lambda i, k: (i, 0))],
        out_shape=[jax.ShapeDtypeStruct((M, N), F32),
                   jax.ShapeDtypeStruct((M, N), BF16)],
        compiler_params=_cparams(("parallel", "arbitrary")),
        name="outproj",
    )(ym, yg, w, x, g)


def _conv_silu(raw_ref, x_ref, shift_ref, w_ref, b_ref, L):
    raw = raw_ref[...]
    x_ref[L:2 * L, :] = raw
    acc = b_ref[...] + w_ref[QK_CONV - 1:QK_CONV, :] * raw.astype(F32)
    for j in range(QK_CONV - 1):
        acc = acc + w_ref[j:j + 1, :] * _dot(shift_ref[j], x_ref[...])
    x_ref[0:L, :] = raw
    return acc * jax.nn.sigmoid(acc)


def _row_to_col(row, eye):
    L = eye.shape[0]
    return jnp.sum(jnp.where(eye, jnp.broadcast_to(row, (L, L)), 0.0), axis=1, keepdims=True)


def _mlstm_kernel(bias_ref, q_ref, k_ref, v_ref, o_ref, g_ref, sh_ref, cwq_ref, cwk_ref, cbq_ref,
                  cbk_ref, ng_ref, out_ref, C_ref, n_ref, m_ref, xq_ref, xk_ref, *, L, DK, DV):
    H = HEADS
    c = pl.program_id(1)

    @pl.when(c == 0)
    def _():
        C_ref[...] = jnp.zeros_like(C_ref)
        n_ref[...] = jnp.zeros_like(n_ref)
        m_ref[...] = jnp.zeros_like(m_ref)
        xq_ref[0:L, :] = jnp.zeros((L, H * DK), BF16)
        xk_ref[0:L, :] = jnp.zeros((L, H * DK), BF16)

    q_all = _conv_silu(q_ref, xq_ref, sh_ref, cwq_ref, cbq_ref, L) * (DK ** -0.5)
    k_all = _conv_silu(k_ref, xk_ref, sh_ref, cwk_ref, cbk_ref, L)

    G = g_ref[...] + bias_ref[...]
    bc = _log_sigmoid(G)
    lane = lax.broadcasted_iota(jnp.int32, (2 * H, L), 1)
    s = 1
    while s < L:
        bc = bc + jnp.where(lane >= s, pltpu.roll(bc, s, axis=1), 0.0)
        s *= 2

    row = lax.broadcasted_iota(jnp.int32, (L, L), 0)
    col = lax.broadcasted_iota(jnp.int32, (L, L), 1)
    eye = row == col
    causal = col <= row

    for h in range(H):
        q = q_all[:, h * DK:(h + 1) * DK]
        k = k_all[:, h * DK:(h + 1) * DK]
        v = v_ref[:, h * DV:(h + 1) * DV]
        gi = G[h:h + 1, :]
        b_row = bc[H + h:H + h + 1, :]
        b_col = _row_to_col(b_row, eye)

        m_prev = m_ref[h]
        C = C_ref[h]
        n = n_ref[h]
        D = jnp.where(causal, b_col - b_row + gi, NEG_INF)
        inter = b_col + m_prev
        m_t = jnp.maximum(inter, jnp.max(D, axis=1, keepdims=True))
        g = jnp.exp(inter - m_t)
        qb = q.astype(BF16)
        kb = k.astype(BF16)
        s_qk = _dot_nt(qb, kb) * jnp.exp(D - m_t)
        num = g * _dot(qb, C.astype(BF16)) + _dot(s_qk.astype(BF16), v)
        den = (g * jnp.sum(q * n, axis=1, keepdims=True)
               + jnp.sum(s_qk, axis=1, keepdims=True))
        hh = num / jnp.maximum(jnp.abs(den), jnp.exp(-m_t))

        y = (_rms(hh) * ng_ref[:, h * DV:(h + 1) * DV]
             * jax.nn.sigmoid(o_ref[:, h * DV:(h + 1) * DV].astype(F32)))
        out_ref[:, h * DV:(h + 1) * DV] = y.astype(out_ref.dtype)

        bL = b_row[:, L - 1:L]
        a_row = bL - b_row + gi
        m_new = jnp.maximum(bL + m_prev, jnp.max(a_row, axis=1, keepdims=True))
        ws_col = _row_to_col(jnp.exp(a_row - m_new), eye)
        gC = jnp.exp(bL + m_prev - m_new)
        kw = k * ws_col
        C_ref[h] = gC * C + _dot_tn(kw.astype(BF16), v)
        n_ref[h] = gC * n + jnp.sum(kw, axis=0, keepdims=True)
        m_ref[h] = m_new


def _mlstm(proj, gates, bias, conv_w, conv_b, norm_g, *, B, S, DK, DV, L):
    H = HEADS
    NC = S // L
    T = B * S
    kern = functools.partial(_mlstm_kernel, L=L, DK=DK, DV=DV)
    rowblk = lambda b, c: b * NC + c
    t_idx = jnp.arange(L)[None, :, None]
    r_idx = jnp.arange(2 * L)[None, None, :]
    lag = (QK_CONV - 1 - jnp.arange(QK_CONV - 1))[:, None, None]
    shifts = (r_idx == L + t_idx - lag).astype(BF16)
    return pl.pallas_call(
        kern,
        grid=(B, NC),
        in_specs=[
            pl.BlockSpec((2 * H, 1), lambda b, c: (0, 0)),
            pl.BlockSpec((L, H * DK), lambda b, c: (rowblk(b, c), 0)),
            pl.BlockSpec((L, H * DK), lambda b, c: (rowblk(b, c), 1)),
            pl.BlockSpec((L, H * DV), lambda b, c: (rowblk(b, c), 1)),
            pl.BlockSpec((L, H * DV), lambda b, c: (rowblk(b, c), 2)),
            pl.BlockSpec((None, None, 2 * H, L), lambda b, c: (b, c, 0, 0)),
            pl.BlockSpec((QK_CONV - 1, L, 2 * L), lambda b, c: (0, 0, 0)),
            pl.BlockSpec((QK_CONV, H * DK), lambda b, c: (0, 0)),
            pl.BlockSpec((QK_CONV, H * DK), lambda b, c: (0, 1)),
            pl.BlockSpec((1, H * DK), lambda b, c: (0, 0)),
            pl.BlockSpec((1, H * DK), lambda b, c: (0, 1)),
            pl.BlockSpec((1, H * DV), lambda b, c: (0, 0)),
        ],
        out_specs=pl.BlockSpec((L, H * DV), lambda b, c: (rowblk(b, c), 0)),
        out_shape=jax.ShapeDtypeStruct((T, H * DV), BF16),
        scratch_shapes=[
            pltpu.VMEM((H, DK, DV), F32),
            pltpu.VMEM((H, 1, DK), F32),
            pltpu.VMEM((H, 1, 1), F32),
            pltpu.VMEM((2 * L, H * DK), BF16),
            pltpu.VMEM((2 * L, H * DK), BF16),
        ],
        compiler_params=_cparams(("parallel", "arbitrary")),
        name="mlstm_scan",
    )(bias, proj, proj, proj, proj, gates, shifts, conv_w, conv_w, conv_b, conv_b, norm_g)


def _gla_kernel(q_ref, k_ref, v_ref, gg_ref, sm_ref, w2_ref, ab_ref, ng_ref, out_ref,
                ST_ref, bs_ref, *, L, DK, DV):
    H = HEADS
    c = pl.program_id(1)

    @pl.when(c == 0)
    def _():
        ST_ref[...] = jnp.zeros_like(ST_ref)
        bs_ref[...] = jnp.zeros_like(bs_ref)

    z = _dot_hi(sm_ref[...], w2_ref[...]) + ab_ref[...]
    la = _log_sigmoid(z) * G_TAU_INV
    row = lax.broadcasted_iota(jnp.int32, (L, L), 0)
    col = lax.broadcasted_iota(jnp.int32, (L, L), 1)
    tril = jnp.where(col <= row, 1.0, 0.0).astype(F32)
    b_all = _dot_hi(tril, la)
    bs_ref[HALO:HALO + L, :] = b_all
    pos_col = lax.broadcasted_iota(jnp.int32, (L, DK), 0)
    eye = row == col

    for h in range(H):
        dk = slice(h * DK, (h + 1) * DK)
        dv = slice(h * DV, (h + 1) * DV)
        b = b_all[:, dk]
        q = q_ref[:, dk].astype(F32) * (DK ** -0.5)
        k = k_ref[:, dk].astype(F32)
        v = v_ref[:, dv]
        ST = ST_ref[h]
        o = _dot_nt((q * jnp.exp(b)).astype(BF16), ST.astype(BF16))

        A = jnp.where(eye, _dot_nt(q.astype(BF16), k.astype(BF16)), 0.0)
        n = L
        while n >= 2:
            hn = n // 2
            pos = pos_col % n
            if n >= 8:
                R = jnp.concatenate(
                    [jnp.broadcast_to(b[i * n + hn - 1:i * n + hn, :], (n, DK))
                     for i in range(L // n)], axis=0)
            elif n == 4:
                R = jnp.where(pos == 0, bs_ref[HALO + 1:HALO + 1 + L, dk],
                              jnp.where(pos == 1, b,
                                        jnp.where(pos == 2, bs_ref[HALO - 1:HALO - 1 + L, dk],
                                                  bs_ref[HALO - 2:HALO - 2 + L, dk])))
            else:
                R = jnp.where(pos == 0, b, bs_ref[HALO - 1:HALO - 1 + L, dk])
            qn = q * jnp.exp(jnp.where(pos >= hn, b - R, NEG_INF))
            kn = k * jnp.exp(jnp.where(pos < hn, R - b, NEG_INF))
            S = _dot_nt(qn.astype(BF16), kn.astype(BF16))
            A = A + jnp.where((row // n) == (col // n), S, 0.0)
            n = hn
        o = o + _dot(A.astype(BF16), v)

        gg = gg_ref[:, dv].astype(F32)
        y = _rms(o) * ng_ref[:, dv] * (gg * jax.nn.sigmoid(gg))
        out_ref[:, dv] = y.astype(out_ref.dtype)

        bL = b[L - 1:L, :]
        kdec = k * jnp.exp(bL - b)
        ST_ref[h] = ST * jnp.exp(bL) + _dot_tn(v, kdec.astype(BF16))


def _gla(proj, small, w2p, a_b, norm_g, *, B, S, DK, DV, L):
    H = HEADS
    NC = S // L
    T = B * S
    kern = functools.partial(_gla_kernel, L=L, DK=DK, DV=DV)
    rowblk = lambda b, c: b * NC + c
    return pl.pallas_call(
        kern,
        grid=(B, NC),
        in_specs=[
            pl.BlockSpec((L, H * DK), lambda b, c: (rowblk(b, c), 0)),
            pl.BlockSpec((L, H * DK), lambda b, c: (rowblk(b, c), 1)),
            pl.BlockSpec((L, H * DV), lambda b, c: (rowblk(b, c), 1)),
            pl.BlockSpec((L, H * DV), lambda b, c: (rowblk(b, c), 2)),
            pl.BlockSpec((L, SMALL_W), lambda b, c: (rowblk(b, c), 0)),
            pl.BlockSpec((SMALL_W, H * DK), lambda b, c: (0, 0)),
            pl.BlockSpec((1, H * DK), lambda b, c: (0, 0)),
            pl.BlockSpec((1, H * DV), lambda b, c: (0, 0)),
        ],
        out_specs=pl.BlockSpec((L, H * DV), lambda b, c: (rowblk(b, c), 0)),
        out_shape=jax.ShapeDtypeStruct((T, H * DV), BF16),
        scratch_shapes=[
            pltpu.VMEM((H, DV, DK), F32),
            pltpu.VMEM((L + 2 * HALO, H * DK), F32),
        ],
        compiler_params=_cparams(("parallel", "arbitrary")),
        name="gla_scan",
    )(proj, proj, proj, proj, small, w2p, a_b, norm_g)


def _ffn_up_kernel(h_ref, wg_ref, wu_ref, cw_ref, cb_ref, act_ref, raw_ref, gbuf_ref, ghalo_ref,
                   *, tm, S, tg, groups_last):
    i = pl.program_id(0)
    j = pl.program_id(1)
    nb = pl.num_programs(1) - 1
    ng = act_ref.shape[1] // tg
    seq_start = (i * tm) % S == 0

    @pl.when(jnp.logical_and(i == 0, j == 0))
    def _():
        ghalo_ref[...] = jnp.zeros_like(ghalo_ref)

    def epilogue(n_groups, anchor):
        jb = j - 1
        for g in range(n_groups):
            cols = slice(g * tg, (g + 1) * tg)
            gate = raw_ref[0, :, cols]
            up = raw_ref[1, :, cols]
            gbuf_ref[g, 0:HALO, :] = jnp.where(seq_start, 0.0, ghalo_ref[jb, :, cols])
            gbuf_ref[g, HALO:HALO + tm, :] = gate
            ghalo_ref[jb, :, cols] = gate[tm - HALO:tm, :]
            conv = (cb_ref[:, cols] + anchor) + cw_ref[FFN_CONV - 1:FFN_CONV, cols] * gate
            for t in range(FFN_CONV - 1):
                off = HALO - (FFN_CONV - 1) + t
                conv = conv + cw_ref[t:t + 1, cols] * gbuf_ref[g, off:off + tm, :]
            act_ref[:, cols] = ((conv * jax.nn.sigmoid(conv)) * up).astype(act_ref.dtype)

    def dots(g):
        cols = slice(g * tg, (g + 1) * tg)
        hb = h_ref[...]
        return _dot(hb, wg_ref[:, cols]), _dot(hb, wu_ref[:, cols])

    def park(g, res):
        cols = slice(g * tg, (g + 1) * tg)
        raw_ref[0, :, cols] = res[0]
        raw_ref[1, :, cols] = res[1]

    def step(n_epilogue, n_matmul):
        res0 = dots(0) if n_matmul else None
        if n_epilogue:
            if n_matmul > 1:
                bits = pltpu.bitcast(res0[0][0:1, :], jnp.uint32)
                anchor = pltpu.bitcast(lax.shift_right_logical(bits, jnp.uint32(32)), F32)
            else:
                anchor = jnp.zeros((1, tg), F32)
            epilogue(n_epilogue, anchor)
        if n_matmul:
            park(0, res0)
        for g in range(1, n_matmul):
            park(g, dots(g))

    pl.when(j == 0)(functools.partial(step, 0, ng))
    if groups_last == ng:
        pl.when(jnp.logical_and(j > 0, j < nb))(functools.partial(step, ng, ng))
    else:
        pl.when(jnp.logical_and(j > 0, j < nb - 1))(functools.partial(step, ng, ng))
        pl.when(j == nb - 1)(functools.partial(step, ng, groups_last))
    pl.when(j == nb)(functools.partial(step, groups_last, 0))


def _ffn_up(h2, wg, wu, conv_w, conv_b, *, S):
    T, D = h2.shape
    F = wg.shape[1]
    tm = _pick(S, (1024, 512, 256, 128))
    tg = _pick(F, (MXU_COLS_V7X, 128))
    tn = FFN_TN
    nb = -(-F // tn)
    groups_last = (F - (nb - 1) * tn) // tg
    kern = functools.partial(_ffn_up_kernel, tm=tm, S=S, tg=tg, groups_last=groups_last)
    cur = lambda i, j: (0, jnp.minimum(j, nb - 1))
    prev = lambda i, j: (0, jnp.maximum(j - 1, 0))
    return pl.pallas_call(
        kern,
        grid=(T // tm, nb + 1),
        in_specs=[
            pl.BlockSpec((tm, D), lambda i, j: (i, 0)),
            pl.BlockSpec((D, tn), cur),
            pl.BlockSpec((D, tn), cur),
            pl.BlockSpec((FFN_CONV, tn), prev),
            pl.BlockSpec((1, tn), prev),
        ],
        out_specs=pl.BlockSpec((tm, tn), lambda i, j: (i, jnp.maximum(j - 1, 0))),
        out_shape=jax.ShapeDtypeStruct((T, F), BF16),
        scratch_shapes=[pltpu.VMEM((2, tm, tn), F32),
                        pltpu.VMEM((tn // tg, tm + HALO, tg), F32),
                        pltpu.VMEM((nb, HALO, tn), F32)],
        compiler_params=_cparams(("arbitrary", "arbitrary")),
        name="ffn_up",
    )(h2, wg, wu, conv_w, conv_b)


def _ffn_down_kernel(a_ref, wd_ref, x_ref, gf_ref, out_ref, *, tn, k_tail):
    k = pl.program_id(1)
    nk = pl.num_programs(1)
    tk = a_ref.shape[1]
    D = out_ref.shape[1]

    @pl.when(k == 0)
    def _():
        out_ref[...] = x_ref[...]

    def accumulate(kk):
        a = a_ref[:, :kk]
        for n0 in range(0, D, tn):
            out_ref[:, n0:n0 + tn] += _dot(a, wd_ref[:kk, n0:n0 + tn])

    if k_tail == tk:
        accumulate(tk)
    else:
        pl.when(k < nk - 1)(lambda: accumulate(tk))
        pl.when(k == nk - 1)(lambda: accumulate(k_tail))

    @pl.when(k == nk - 1)
    def _():
        out_ref[...] = _rms(out_ref[...]) * gf_ref[...]


def _ffn_down(act, wd, x1, lnf_g):
    T, F = act.shape
    D = wd.shape[1]
    tm = _pick(T, (512, 256, 128))
    tn = _pick(D, (1024, 512, 256, 128))
    tk = FFN_TK
    nk = -(-F // tk)
    kern = functools.partial(_ffn_down_kernel, tn=tn, k_tail=F - (nk - 1) * tk)
    return pl.pallas_call(
        kern,
        grid=(T // tm, nk),
        in_specs=[
            pl.BlockSpec((tm, tk), lambda i, k: (i, k)),
            pl.BlockSpec((tk, D), lambda i, k: (k, 0)),
            pl.BlockSpec((tm, D), lambda i, k: (i, 0)),
            pl.BlockSpec((1, D), lambda i, k: (0, 0)),
        ],
        out_specs=pl.BlockSpec((tm, D), lambda i, k: (i, 0)),
        out_shape=jax.ShapeDtypeStruct((T, D), F32),
        compiler_params=_cparams(("parallel", "arbitrary")),
        name="ffn_down",
    )(act, wd, x1, lnf_g)


def kernel(x, ln1_g, w_in, mlstm_conv_w, mlstm_conv_b, mlstm_i_b, mlstm_f_b, mlstm_norm_g,
           gla_a_w2, gla_a_b, gla_norm_g, w_out, ln2_g, w_ffn_gate, w_ffn_up,
           ffn_conv_w, ffn_conv_b, w_ffn_down, lnf_g):
    B, S, D = x.shape
    T = B * S
    H = HEADS
    M_DK = mlstm_conv_w.shape[1] // (2 * H)
    M_DV = mlstm_norm_g.shape[0] // H
    G_DK = gla_a_w2.shape[1] // H
    G_DV = gla_norm_g.shape[0] // H
    F = w_ffn_gate.shape[1]
    assert M_DV == 2 * M_DK and G_DV == 2 * G_DK and M_DK == G_DK

    n_m = 2 * H * M_DK + 2 * H * M_DV
    n_g = 2 * H * G_DK + 2 * H * G_DV
    o_gate = n_m
    o_g = n_m + 2 * H
    o_a = o_g + n_g
    assert w_in.shape[1] == o_a + G_RANK

    w_in_t = w_in.T.astype(BF16)
    w_g = w_in_t[o_g:o_a]
    w_small = jnp.concatenate(
        [w_in_t[o_gate:o_g], w_in_t[o_a:],
         jnp.zeros((SMALL_W - 2 * H - G_RANK, D), BF16)], axis=0)

    x2d = x.reshape(T, D)
    h1 = _rmsnorm(x2d, ln1_g, BF16)
    proj_m = _matmul_nt(h1, w_in_t, BF16, "inproj_mlstm", n_out=n_m)
    proj_g = _matmul_nt(h1, w_g, BF16, "inproj_gla")
    small = _matmul_nt(h1, w_small, F32, "inproj_small")

    L_M = 128
    gates = small[:, :2 * H].reshape(B, S // L_M, L_M, 2 * H)
    gates = jnp.transpose(gates, (0, 1, 3, 2))
    bias = jnp.concatenate([mlstm_i_b, mlstm_f_b]).astype(F32).reshape(2 * H, 1)
    ym = _mlstm(proj_m, gates, bias, mlstm_conv_w.astype(F32),
                mlstm_conv_b.reshape(1, -1).astype(F32), mlstm_norm_g.reshape(1, -1).astype(F32),
                B=B, S=S, DK=M_DK, DV=M_DV, L=L_M)

    L_G = 128
    w2p = jnp.zeros((SMALL_W, H * G_DK), F32).at[2 * H:2 * H + G_RANK].set(gla_a_w2.astype(F32))
    yg = _gla(proj_g, small, w2p, gla_a_b.reshape(1, -1).astype(F32),
              gla_norm_g.reshape(1, -1).astype(F32), B=B, S=S, DK=G_DK, DV=G_DV, L=L_G)

    x1, h2 = _outproj(ym, yg, w_out.astype(BF16), x2d, ln2_g.reshape(1, -1).astype(F32))

    act = _ffn_up(h2, w_ffn_gate.astype(BF16), w_ffn_up.astype(BF16), ffn_conv_w.astype(F32),
                  ffn_conv_b.reshape(1, -1).astype(F32), S=S)
    out = _ffn_down(act, w_ffn_down.astype(BF16), x1, lnf_g.reshape(1, -1).astype(F32))
    return out.reshape(B, S, D)
```

```python
import functools

import jax
import jax.numpy as jnp
from jax import lax
from jax.experimental import pallas as pl
from jax.experimental.pallas import tpu as pltpu

F32 = jnp.float32
BF16 = jnp.bfloat16

HEADS = 4
QK_CONV = 4
FFN_CONV = 3
G_RANK = 16
G_TAU_INV = 1.0 / 16.0
EPS = 1e-6
SMALL_W = 128
HALO = 8
NEG_INF = float("-inf")
FFN_TN = 512
FFN_TK = 1024
MXU_COLS_V7X = 256

VMEM_LIMIT_V7X = 56 * 1024 * 1024


def _cparams(sem, vmem=VMEM_LIMIT_V7X):
    return pltpu.CompilerParams(dimension_semantics=sem, vmem_limit_bytes=vmem)


def _pick(n, prefs):
    for p in prefs:
        if n % p == 0:
            return p
    raise ValueError(f"no tile in {prefs} divides {n}")


def _log_sigmoid(x):
    return jnp.minimum(x, 0.0) - jnp.log1p(jnp.exp(-jnp.abs(x)))


def _dot(a, b):
    return jnp.dot(a, b, preferred_element_type=F32)


def _dot_hi(a, b):
    return jnp.dot(a, b, preferred_element_type=F32, precision=lax.Precision.HIGHEST)


def _dot_nt(a, b):
    return lax.dot_general(a, b, (((1,), (1,)), ((), ())), preferred_element_type=F32)


def _dot_tn(a, b):
    return lax.dot_general(a, b, (((0,), (0,)), ((), ())), preferred_element_type=F32)


def _rms(x):
    return x * lax.rsqrt(jnp.mean(x * x, axis=-1, keepdims=True) + EPS)


def _rmsnorm_kernel(x_ref, g_ref, o_ref):
    o_ref[...] = (_rms(x_ref[...].astype(F32)) * g_ref[...]).astype(o_ref.dtype)


def _rmsnorm(x, g, out_dtype):
    T, D = x.shape
    tr = _pick(T, (512, 256, 128, 64, 8))
    return pl.pallas_call(
        _rmsnorm_kernel,
        grid=(T // tr,),
        in_specs=[pl.BlockSpec((tr, D), lambda i: (i, 0)),
                  pl.BlockSpec((1, D), lambda i: (0, 0))],
        out_specs=pl.BlockSpec((tr, D), lambda i: (i, 0)),
        out_shape=jax.ShapeDtypeStruct((T, D), out_dtype),
        compiler_params=_cparams(("parallel",)),
        name="rmsnorm",
    )(x, g.reshape(1, D).astype(F32))


def _matmul_nt_kernel(a_ref, wt_ref, o_ref):
    o_ref[...] = _dot_nt(a_ref[...], wt_ref[...]).astype(o_ref.dtype)


def _matmul_nt(a, wt, out_dtype, name, n_out=None):
    M, K = a.shape
    N = wt.shape[0] if n_out is None else n_out
    tm = _pick(M, (1024, 512, 256, 128))
    tn = _pick(N, (1024, 512, 256, 128))
    return pl.pallas_call(
        _matmul_nt_kernel,
        grid=(M // tm, N // tn),
        in_specs=[pl.BlockSpec((tm, K), lambda i, j: (i, 0)),
                  pl.BlockSpec((tn, K), lambda i, j: (j, 0))],
        out_specs=pl.BlockSpec((tm, tn), lambda i, j: (i, j)),
        out_shape=jax.ShapeDtypeStruct((M, N), out_dtype),
        compiler_params=_cparams(("parallel", "arbitrary")),
        name=name,
    )(a, wt)


def _outproj_kernel(ym_ref, yg_ref, wt_ref, wb_ref, x_ref, o_ref):
    acc = _dot(ym_ref[...], wt_ref[...]) + _dot(yg_ref[...], wb_ref[...])
    o_ref[...] = x_ref[...] + acc


def _outproj(ym, yg, w, x):
    M, K2 = ym.shape
    N = w.shape[1]
    tm = _pick(M, (1024, 512, 256, 128))
    tn = _pick(N, (1024, 512, 256, 128))
    return pl.pallas_call(
        _outproj_kernel,
        grid=(M // tm, N // tn),
        in_specs=[pl.BlockSpec((tm, K2), lambda i, j: (i, 0)),
                  pl.BlockSpec((tm, K2), lambda i, j: (i, 0)),
                  pl.BlockSpec((K2, tn), lambda i, j: (0, j)),
                  pl.BlockSpec((K2, tn), lambda i, j: (1, j)),
                  pl.BlockSpec((tm, tn), lambda i, j: (i, j))],
        out_specs=pl.BlockSpec((tm, tn), lambda i, j: (i, j)),
        out_shape=jax.ShapeDtypeStruct((M, N), F32),
        compiler_params=_cparams(("parallel", "arbitrary")),
        name="outproj",
    )(ym, yg, w, w, x)


def _conv_silu(raw_ref, x_ref, shift_ref, w_ref, b_ref, L):
    raw = raw_ref[...]
    x_ref[L:2 * L, :] = raw
    acc = b_ref[...] + w_ref[QK_CONV - 1:QK_CONV, :] * raw.astype(F32)
    for j in range(QK_CONV - 1):
        acc = acc + w_ref[j:j + 1, :] * _dot(shift_ref[j], x_ref[...])
    x_ref[0:L, :] = raw
    return acc * jax.nn.sigmoid(acc)


def _row_to_col(row, eye):
    L = eye.shape[0]
    return jnp.sum(jnp.where(eye, jnp.broadcast_to(row, (L, L)), 0.0), axis=1, keepdims=True)


def _mlstm_init(C_ref, n_ref, m_ref, xq_ref, xk_ref):
    L = xq_ref.shape[0] // 2
    C_ref[...] = jnp.zeros_like(C_ref)
    n_ref[...] = jnp.zeros_like(n_ref)
    m_ref[...] = jnp.zeros_like(m_ref)
    xq_ref[0:L, :] = jnp.zeros((L, xq_ref.shape[1]), BF16)
    xk_ref[0:L, :] = jnp.zeros((L, xk_ref.shape[1]), BF16)


def _mlstm_body(bias_ref, q_ref, k_ref, v_ref, o_ref, g_ref, sh_ref, cwq_ref, cwk_ref, cbq_ref,
                cbk_ref, ng_ref, out_ref, C_ref, n_ref, m_ref, xq_ref, xk_ref, *, L, DK, DV):
    H = HEADS
    q_all = _conv_silu(q_ref, xq_ref, sh_ref, cwq_ref, cbq_ref, L) * (DK ** -0.5)
    k_all = _conv_silu(k_ref, xk_ref, sh_ref, cwk_ref, cbk_ref, L)

    G = g_ref[...] + bias_ref[...]
    bc = _log_sigmoid(G)
    lane = lax.broadcasted_iota(jnp.int32, (2 * H, L), 1)
    s = 1
    while s < L:
        bc = bc + jnp.where(lane >= s, pltpu.roll(bc, s, axis=1), 0.0)
        s *= 2

    row = lax.broadcasted_iota(jnp.int32, (L, L), 0)
    col = lax.broadcasted_iota(jnp.int32, (L, L), 1)
    eye = row == col
    causal = col <= row

    for h in range(H):
        q = q_all[:, h * DK:(h + 1) * DK]
        k = k_all[:, h * DK:(h + 1) * DK]
        v = v_ref[:, h * DV:(h + 1) * DV]
        gi = G[h:h + 1, :]
        b_row = bc[H + h:H + h + 1, :]
        b_col = _row_to_col(b_row, eye)

        m_prev = m_ref[h]
        C = C_ref[h]
        n = n_ref[h]
        D = jnp.where(causal, b_col - b_row + gi, NEG_INF)
        inter = b_col + m_prev
        m_t = jnp.maximum(inter, jnp.max(D, axis=1, keepdims=True))
        g = jnp.exp(inter - m_t)
        qb = q.astype(BF16)
        kb = k.astype(BF16)
        s_qk = _dot_nt(qb, kb) * jnp.exp(D - m_t)
        num = g * _dot(qb, C.astype(BF16)) + _dot(s_qk.astype(BF16), v)
        den = (g * jnp.sum(q * n, axis=1, keepdims=True)
               + jnp.sum(s_qk, axis=1, keepdims=True))
        hh = num / jnp.maximum(jnp.abs(den), jnp.exp(-m_t))

        y = (_rms(hh) * ng_ref[:, h * DV:(h + 1) * DV]
             * jax.nn.sigmoid(o_ref[:, h * DV:(h + 1) * DV].astype(F32)))
        out_ref[:, h * DV:(h + 1) * DV] = y.astype(out_ref.dtype)

        bL = b_row[:, L - 1:L]
        a_row = bL - b_row + gi
        m_new = jnp.maximum(bL + m_prev, jnp.max(a_row, axis=1, keepdims=True))
        ws_col = _row_to_col(jnp.exp(a_row - m_new), eye)
        gC = jnp.exp(bL + m_prev - m_new)
        kw = k * ws_col
        C_ref[h] = gC * C + _dot_tn(kw.astype(BF16), v)
        n_ref[h] = gC * n + jnp.sum(kw, axis=0, keepdims=True)
        m_ref[h] = m_new


def _gla_init(ST_ref, bs_ref):
    ST_ref[...] = jnp.zeros_like(ST_ref)
    bs_ref[...] = jnp.zeros_like(bs_ref)


def _gla_body(q_ref, k_ref, v_ref, gg_ref, sm_ref, w2_ref, ab_ref, ng_ref, out_ref,
              ST_ref, bs_ref, *, L, DK, DV):
    H = HEADS
    z = _dot_hi(sm_ref[...], w2_ref[...]) + ab_ref[...]
    la = _log_sigmoid(z) * G_TAU_INV
    row = lax.broadcasted_iota(jnp.int32, (L, L), 0)
    col = lax.broadcasted_iota(jnp.int32, (L, L), 1)
    tril = jnp.where(col <= row, 1.0, 0.0).astype(F32)
    b_all = _dot_hi(tril, la)
    bs_ref[HALO:HALO + L, :] = b_all
    pos_col = lax.broadcasted_iota(jnp.int32, (L, DK), 0)
    eye = row == col

    for h in range(H):
        dk = slice(h * DK, (h + 1) * DK)
        dv = slice(h * DV, (h + 1) * DV)
        b = b_all[:, dk]
        q = q_ref[:, dk].astype(F32) * (DK ** -0.5)
        k = k_ref[:, dk].astype(F32)
        v = v_ref[:, dv]
        ST = ST_ref[h]
        o = _dot_nt((q * jnp.exp(b)).astype(BF16), ST.astype(BF16))

        A = jnp.where(eye, _dot_nt(q.astype(BF16), k.astype(BF16)), 0.0)
        n = L
        while n >= 2:
            hn = n // 2
            pos = pos_col % n
            if n >= 8:
                R = jnp.concatenate(
                    [jnp.broadcast_to(b[i * n + hn - 1:i * n + hn, :], (n, DK))
                     for i in range(L // n)], axis=0)
            elif n == 4:
                R = jnp.where(pos == 0, bs_ref[HALO + 1:HALO + 1 + L, dk],
                              jnp.where(pos == 1, b,
                                        jnp.where(pos == 2, bs_ref[HALO - 1:HALO - 1 + L, dk],
                                                  bs_ref[HALO - 2:HALO - 2 + L, dk])))
            else:
                R = jnp.where(pos == 0, b, bs_ref[HALO - 1:HALO - 1 + L, dk])
            qn = q * jnp.exp(jnp.where(pos >= hn, b - R, NEG_INF))
            kn = k * jnp.exp(jnp.where(pos < hn, R - b, NEG_INF))
            S = _dot_nt(qn.astype(BF16), kn.astype(BF16))
            A = A + jnp.where((row // n) == (col // n), S, 0.0)
            n = hn
        o = o + _dot(A.astype(BF16), v)

        gg = gg_ref[:, dv].astype(F32)
        y = _rms(o) * ng_ref[:, dv] * (gg * jax.nn.sigmoid(gg))
        out_ref[:, dv] = y.astype(out_ref.dtype)

        bL = b[L - 1:L, :]
        kdec = k * jnp.exp(bL - b)
        ST_ref[h] = ST * jnp.exp(bL) + _dot_tn(v, kdec.astype(BF16))


N_MLSTM_IN, N_GLA_IN, N_MLSTM_SCRATCH = 12, 8, 5


def _scan_kernel(*refs, L, DK, DV):
    m_in = refs[:N_MLSTM_IN]
    g_in = refs[N_MLSTM_IN:N_MLSTM_IN + N_GLA_IN]
    ym_ref, yg_ref = refs[N_MLSTM_IN + N_GLA_IN:N_MLSTM_IN + N_GLA_IN + 2]
    scratch = refs[N_MLSTM_IN + N_GLA_IN + 2:]
    m_scr, g_scr = scratch[:N_MLSTM_SCRATCH], scratch[N_MLSTM_SCRATCH:]

    @pl.when(pl.program_id(1) == 0)
    def _():
        _mlstm_init(*m_scr)
        _gla_init(*g_scr)

    _mlstm_body(*m_in, ym_ref, *m_scr, L=L, DK=DK, DV=DV)
    _gla_body(*g_in, yg_ref, *g_scr, L=L, DK=DK, DV=DV)


def _scans(proj_m, gates, bias, conv_w, conv_b, norm_m, proj_g, small, w2p, a_b, norm_g,
           *, B, S, DK, DV, L):
    H = HEADS
    NC = S // L
    T = B * S
    kern = functools.partial(_scan_kernel, L=L, DK=DK, DV=DV)
    rowblk = lambda b, c: b * NC + c
    t_idx = jnp.arange(L)[None, :, None]
    r_idx = jnp.arange(2 * L)[None, None, :]
    lag = (QK_CONV - 1 - jnp.arange(QK_CONV - 1))[:, None, None]
    shifts = (r_idx == L + t_idx - lag).astype(BF16)
    qk = lambda j: pl.BlockSpec((L, H * DK), lambda b, c: (rowblk(b, c), j))
    vo = lambda j: pl.BlockSpec((L, H * DV), lambda b, c: (rowblk(b, c), j))
    fixed = lambda shape: pl.BlockSpec(shape, lambda b, c: (0,) * len(shape))
    out_spec = pl.BlockSpec((L, H * DV), lambda b, c: (rowblk(b, c), 0))
    out_sds = jax.ShapeDtypeStruct((T, H * DV), BF16)
    return pl.pallas_call(
        kern,
        grid=(B, NC),
        in_specs=[
            fixed((2 * H, 1)),
            qk(0), qk(1), vo(1), vo(2),
            pl.BlockSpec((None, None, 2 * H, L), lambda b, c: (b, c, 0, 0)),
            fixed((QK_CONV - 1, L, 2 * L)),
            pl.BlockSpec((QK_CONV, H * DK), lambda b, c: (0, 0)),
            pl.BlockSpec((QK_CONV, H * DK), lambda b, c: (0, 1)),
            pl.BlockSpec((1, H * DK), lambda b, c: (0, 0)),
            pl.BlockSpec((1, H * DK), lambda b, c: (0, 1)),
            fixed((1, H * DV)),
            qk(0), qk(1), vo(1), vo(2),
            pl.BlockSpec((L, SMALL_W), lambda b, c: (rowblk(b, c), 0)),
            fixed((SMALL_W, H * DK)),
            fixed((1, H * DK)),
            fixed((1, H * DV)),
        ],
        out_specs=[out_spec, out_spec],
        out_shape=[out_sds, out_sds],
        scratch_shapes=[
            pltpu.VMEM((H, DK, DV), F32),
            pltpu.VMEM((H, 1, DK), F32),
            pltpu.VMEM((H, 1, 1), F32),
            pltpu.VMEM((2 * L, H * DK), BF16),
            pltpu.VMEM((2 * L, H * DK), BF16),
            pltpu.VMEM((H, DV, DK), F32),
            pltpu.VMEM((L + 2 * HALO, H * DK), F32),
        ],
        compiler_params=_cparams(("parallel", "arbitrary")),
        name="scans",
    )(bias, proj_m, proj_m, proj_m, proj_m, gates, shifts, conv_w, conv_w, conv_b, conv_b, norm_m,
      proj_g, proj_g, proj_g, proj_g, small, w2p, a_b, norm_g)


def _ffn_up_kernel(h_ref, wg_ref, wu_ref, cw_ref, cb_ref, act_ref, raw_ref, gbuf_ref, ghalo_ref,
                   *, tm, S, tg, groups_last):
    i = pl.program_id(0)
    j = pl.program_id(1)
    nb = pl.num_programs(1) - 1
    ng = act_ref.shape[1] // tg
    seq_start = (i * tm) % S == 0

    @pl.when(jnp.logical_and(i == 0, j == 0))
    def _():
        ghalo_ref[...] = jnp.zeros_like(ghalo_ref)

    def epilogue(n_groups, anchor):
        jb = j - 1
        for g in range(n_groups):
            cols = slice(g * tg, (g + 1) * tg)
            gate = raw_ref[0, :, cols]
            up = raw_ref[1, :, cols]
            gbuf_ref[g, 0:HALO, :] = jnp.where(seq_start, 0.0, ghalo_ref[jb, :, cols])
            gbuf_ref[g, HALO:HALO + tm, :] = gate
            ghalo_ref[jb, :, cols] = gate[tm - HALO:tm, :]
            conv = (cb_ref[:, cols] + anchor) + cw_ref[FFN_CONV - 1:FFN_CONV, cols] * gate
            for t in range(FFN_CONV - 1):
                off = HALO - (FFN_CONV - 1) + t
                conv = conv + cw_ref[t:t + 1, cols] * gbuf_ref[g, off:off + tm, :]
            act_ref[:, cols] = ((conv * jax.nn.sigmoid(conv)) * up).astype(act_ref.dtype)

    def dots(g):
        cols = slice(g * tg, (g + 1) * tg)
        hb = h_ref[...]
        return _dot(hb, wg_ref[:, cols]), _dot(hb, wu_ref[:, cols])

    def park(g, res):
        cols = slice(g * tg, (g + 1) * tg)
        raw_ref[0, :, cols] = res[0]
        raw_ref[1, :, cols] = res[1]

    def step(n_epilogue, n_matmul):
        res0 = dots(0) if n_matmul else None
        if n_epilogue:
            if n_matmul > 1:
                bits = pltpu.bitcast(res0[0][0:1, :], jnp.uint32)
                anchor = pltpu.bitcast(lax.shift_right_logical(bits, jnp.uint32(32)), F32)
            else:
                anchor = jnp.zeros((1, tg), F32)
            epilogue(n_epilogue, anchor)
        if n_matmul:
            park(0, res0)
        for g in range(1, n_matmul):
            park(g, dots(g))

    pl.when(j == 0)(functools.partial(step, 0, ng))
    if groups_last == ng:
        pl.when(jnp.logical_and(j > 0, j < nb))(functools.partial(step, ng, ng))
    else:
        pl.when(jnp.logical_and(j > 0, j < nb - 1))(functools.partial(step, ng, ng))
        pl.when(j == nb - 1)(functools.partial(step, ng, groups_last))
    pl.when(j == nb)(functools.partial(step, groups_last, 0))


def _ffn_up(h2, wg, wu, conv_w, conv_b, *, S):
    T, D = h2.shape
    F = wg.shape[1]
    tm = _pick(S, (1024, 512, 256, 128))
    tg = _pick(F, (MXU_COLS_V7X, 128))
    tn = FFN_TN
    nb = -(-F // tn)
    groups_last = (F - (nb - 1) * tn) // tg
    kern = functools.partial(_ffn_up_kernel, tm=tm, S=S, tg=tg, groups_last=groups_last)
    cur = lambda i, j: (0, jnp.minimum(j, nb - 1))
    prev = lambda i, j: (0, jnp.maximum(j - 1, 0))
    return pl.pallas_call(
        kern,
        grid=(T // tm, nb + 1),
        in_specs=[
            pl.BlockSpec((tm, D), lambda i, j: (i, 0)),
            pl.BlockSpec((D, tn), cur),
            pl.BlockSpec((D, tn), cur),
            pl.BlockSpec((FFN_CONV, tn), prev),
            pl.BlockSpec((1, tn), prev),
        ],
        out_specs=pl.BlockSpec((tm, tn), lambda i, j: (i, jnp.maximum(j - 1, 0))),
        out_shape=jax.ShapeDtypeStruct((T, F), BF16),
        scratch_shapes=[pltpu.VMEM((2, tm, tn), F32),
                        pltpu.VMEM((tn // tg, tm + HALO, tg), F32),
                        pltpu.VMEM((nb, HALO, tn), F32)],
        compiler_params=_cparams(("arbitrary", "arbitrary")),
        name="ffn_up",
    )(h2, wg, wu, conv_w, conv_b)


def _ffn_down_kernel(a_ref, wd_ref, x_ref, gf_ref, out_ref, *, tn, k_tail):
    k = pl.program_id(1)
    nk = pl.num_programs(1)
    tk = a_ref.shape[1]
    D = out_ref.shape[1]

    @pl.when(k == 0)
    def _():
        out_ref[...] = x_ref[...]

    def accumulate(kk):
        a = a_ref[:, :kk]
        for n0 in range(0, D, tn):
            out_ref[:, n0:n0 + tn] += _dot(a, wd_ref[:kk, n0:n0 + tn])

    if k_tail == tk:
        accumulate(tk)
    else:
        pl.when(k < nk - 1)(lambda: accumulate(tk))
        pl.when(k == nk - 1)(lambda: accumulate(k_tail))

    @pl.when(k == nk - 1)
    def _():
        out_ref[...] = _rms(out_ref[...]) * gf_ref[...]


def _ffn_down(act, wd, x1, lnf_g):
    T, F = act.shape
    D = wd.shape[1]
    tm = _pick(T, (512, 256, 128))
    tn = _pick(D, (1024, 512, 256, 128))
    tk = FFN_TK
    nk = -(-F // tk)
    kern = functools.partial(_ffn_down_kernel, tn=tn, k_tail=F - (nk - 1) * tk)
    return pl.pallas_call(
        kern,
        grid=(T // tm, nk),
        in_specs=[
            pl.BlockSpec((tm, tk), lambda i, k: (i, k)),
            pl.BlockSpec((tk, D), lambda i, k: (k, 0)),
            pl.BlockSpec((tm, D), lambda i, k: (i, 0)),
            pl.BlockSpec((1, D), lambda i, k: (0, 0)),
        ],
        out_specs=pl.BlockSpec((tm, D), lambda i, k: (i, 0)),
        out_shape=jax.ShapeDtypeStruct((T, D), F32),
        compiler_params=_cparams(("parallel", "arbitrary")),
        name="ffn_down",
    )(act, wd, x1, lnf_g)


def kernel(x, ln1_g, w_in, mlstm_conv_w, mlstm_conv_b, mlstm_i_b, mlstm_f_b, mlstm_norm_g,
           gla_a_w2, gla_a_b, gla_norm_g, w_out, ln2_g, w_ffn_gate, w_ffn_up,
           ffn_conv_w, ffn_conv_b, w_ffn_down, lnf_g):
    B, S, D = x.shape
    T = B * S
    H = HEADS
    M_DK = mlstm_conv_w.shape[1] // (2 * H)
    M_DV = mlstm_norm_g.shape[0] // H
    G_DK = gla_a_w2.shape[1] // H
    G_DV = gla_norm_g.shape[0] // H
    F = w_ffn_gate.shape[1]
    assert M_DV == 2 * M_DK and G_DV == 2 * G_DK and M_DK == G_DK

    n_m = 2 * H * M_DK + 2 * H * M_DV
    n_g = 2 * H * G_DK + 2 * H * G_DV
    o_gate = n_m
    o_g = n_m + 2 * H
    o_a = o_g + n_g
    assert w_in.shape[1] == o_a + G_RANK

    w_in_t = w_in.T.astype(BF16)
    w_g = w_in_t[o_g:o_a]
    w_small = jnp.concatenate(
        [w_in_t[o_gate:o_g], w_in_t[o_a:],
         jnp.zeros((SMALL_W - 2 * H - G_RANK, D), BF16)], axis=0)

    x2d = x.reshape(T, D)
    h1 = _rmsnorm(x2d, ln1_g, BF16)
    proj_m = _matmul_nt(h1, w_in_t, BF16, "inproj_mlstm", n_out=n_m)
    proj_g = _matmul_nt(h1, w_g, BF16, "inproj_gla")
    small = _matmul_nt(h1, w_small, F32, "inproj_small")

    L = 128
    gates = small[:, :2 * H].reshape(B, S // L, L, 2 * H)
    gates = jnp.transpose(gates, (0, 1, 3, 2))
    bias = jnp.concatenate([mlstm_i_b, mlstm_f_b]).astype(F32).reshape(2 * H, 1)
    w2p = jnp.zeros((SMALL_W, H * G_DK), F32).at[2 * H:2 * H + G_RANK].set(gla_a_w2.astype(F32))
    ym, yg = _scans(proj_m, gates, bias, mlstm_conv_w.astype(F32),
                    mlstm_conv_b.reshape(1, -1).astype(F32), mlstm_norm_g.reshape(1, -1).astype(F32),
                    proj_g, small, w2p, gla_a_b.reshape(1, -1).astype(F32),
                    gla_norm_g.reshape(1, -1).astype(F32), B=B, S=S, DK=M_DK, DV=M_DV, L=L)

    x1 = _outproj(ym, yg, w_out.astype(BF16), x2d)

    h2 = _rmsnorm(x1, ln2_g, BF16)
    act = _ffn_up(h2, w_ffn_gate.astype(BF16), w_ffn_up.astype(BF16), ffn_conv_w.astype(F32),
                  ffn_conv_b.reshape(1, -1).astype(F32), S=S)
    out = _ffn_down(act, w_ffn_down.astype(BF16), x1, lnf_g.reshape(1, -1).astype(F32))
    return out.reshape(B, S, D)
```

```python
import functools

import jax
import jax.numpy as jnp
from jax import lax
from jax.experimental import pallas as pl
from jax.experimental.pallas import tpu as pltpu

F32 = jnp.float32
BF16 = jnp.bfloat16

HEADS = 4
QK_CONV = 4
FFN_CONV = 3
G_RANK = 16
G_TAU_INV = 1.0 / 16.0
EPS = 1e-6
SMALL_W = 128
HALO = 8
NEG_INF = float("-inf")
FFN_TN = 512
FFN_TK = 1024
MXU_COLS_V7X = 256

VMEM_LIMIT_V7X = 56 * 1024 * 1024


def _cparams(sem, vmem=VMEM_LIMIT_V7X):
    return pltpu.CompilerParams(dimension_semantics=sem, vmem_limit_bytes=vmem)


def _pick(n, prefs):
    for p in prefs:
        if n % p == 0:
            return p
    raise ValueError(f"no tile in {prefs} divides {n}")


def _log_sigmoid(x):
    return jnp.minimum(x, 0.0) - jnp.log1p(jnp.exp(-jnp.abs(x)))


def _dot(a, b):
    return jnp.dot(a, b, preferred_element_type=F32)


def _dot_hi(a, b):
    return jnp.dot(a, b, preferred_element_type=F32, precision=lax.Precision.HIGHEST)


def _dot_nt(a, b):
    return lax.dot_general(a, b, (((1,), (1,)), ((), ())), preferred_element_type=F32)


def _dot_tn(a, b):
    return lax.dot_general(a, b, (((0,), (0,)), ((), ())), preferred_element_type=F32)


def _rms(x):
    return x * lax.rsqrt(jnp.mean(x * x, axis=-1, keepdims=True) + EPS)


def _rmsnorm_kernel(x_ref, g_ref, o_ref):
    o_ref[...] = (_rms(x_ref[...].astype(F32)) * g_ref[...]).astype(o_ref.dtype)


def _rmsnorm(x, g, out_dtype):
    T, D = x.shape
    tr = _pick(T, (512, 256, 128, 64, 8))
    return pl.pallas_call(
        _rmsnorm_kernel,
        grid=(T // tr,),
        in_specs=[pl.BlockSpec((tr, D), lambda i: (i, 0)),
                  pl.BlockSpec((1, D), lambda i: (0, 0))],
        out_specs=pl.BlockSpec((tr, D), lambda i: (i, 0)),
        out_shape=jax.ShapeDtypeStruct((T, D), out_dtype),
        compiler_params=_cparams(("parallel",)),
        name="rmsnorm",
    )(x, g.reshape(1, D).astype(F32))


def _matmul_nt_kernel(a_ref, wt_ref, o_ref):
    o_ref[...] = _dot_nt(a_ref[...], wt_ref[...]).astype(o_ref.dtype)


def _matmul_nt(a, wt, out_dtype, name, n_out=None):
    M, K = a.shape
    N = wt.shape[0] if n_out is None else n_out
    tm = _pick(M, (1024, 512, 256, 128))
    tn = _pick(N, (1024, 512, 256, 128))
    return pl.pallas_call(
        _matmul_nt_kernel,
        grid=(M // tm, N // tn),
        in_specs=[pl.BlockSpec((tm, K), lambda i, j: (i, 0)),
                  pl.BlockSpec((tn, K), lambda i, j: (j, 0))],
        out_specs=pl.BlockSpec((tm, tn), lambda i, j: (i, j)),
        out_shape=jax.ShapeDtypeStruct((M, N), out_dtype),
        compiler_params=_cparams(("parallel", "arbitrary")),
        name=name,
    )(a, wt)


def _outproj_kernel(ym_ref, yg_ref, wt_ref, wb_ref, x_ref, o_ref):
    acc = _dot(ym_ref[...], wt_ref[...]) + _dot(yg_ref[...], wb_ref[...])
    o_ref[...] = x_ref[...] + acc


def _outproj(ym, yg, w, x):
    M, K2 = ym.shape
    N = w.shape[1]
    tm = _pick(M, (1024, 512, 256, 128))
    tn = _pick(N, (1024, 512, 256, 128))
    return pl.pallas_call(
        _outproj_kernel,
        grid=(M // tm, N // tn),
        in_specs=[pl.BlockSpec((tm, K2), lambda i, j: (i, 0)),
                  pl.BlockSpec((tm, K2), lambda i, j: (i, 0)),
                  pl.BlockSpec((K2, tn), lambda i, j: (0, j)),
                  pl.BlockSpec((K2, tn), lambda i, j: (1, j)),
                  pl.BlockSpec((tm, tn), lambda i, j: (i, j))],
        out_specs=pl.BlockSpec((tm, tn), lambda i, j: (i, j)),
        out_shape=jax.ShapeDtypeStruct((M, N), F32),
        compiler_params=_cparams(("parallel", "arbitrary")),
        name="outproj",
    )(ym, yg, w, w, x)


def _conv_silu(raw_ref, x_ref, shift_ref, w_ref, b_ref, L):
    raw = raw_ref[...]
    x_ref[L:2 * L, :] = raw
    acc = b_ref[...] + w_ref[QK_CONV - 1:QK_CONV, :] * raw.astype(F32)
    for j in range(QK_CONV - 1):
        acc = acc + w_ref[j:j + 1, :] * _dot(shift_ref[j], x_ref[...])
    x_ref[0:L, :] = raw
    return acc * jax.nn.sigmoid(acc)


def _row_to_col(row, eye):
    L = eye.shape[0]
    return jnp.sum(jnp.where(eye, jnp.broadcast_to(row, (L, L)), 0.0), axis=1, keepdims=True)


def _mlstm_init(C_ref, n_ref, m_ref, xq_ref, xk_ref):
    L = xq_ref.shape[0] // 2
    C_ref[...] = jnp.zeros_like(C_ref)
    n_ref[...] = jnp.zeros_like(n_ref)
    m_ref[...] = jnp.zeros_like(m_ref)
    xq_ref[0:L, :] = jnp.zeros((L, xq_ref.shape[1]), BF16)
    xk_ref[0:L, :] = jnp.zeros((L, xk_ref.shape[1]), BF16)


def _mlstm_body(bias_ref, q_ref, k_ref, v_ref, o_ref, g_ref, sh_ref, cwq_ref, cwk_ref, cbq_ref,
                cbk_ref, ng_ref, out_ref, C_ref, n_ref, m_ref, xq_ref, xk_ref, *, L, DK, DV):
    H = HEADS
    q_all = _conv_silu(q_ref, xq_ref, sh_ref, cwq_ref, cbq_ref, L) * (DK ** -0.5)
    k_all = _conv_silu(k_ref, xk_ref, sh_ref, cwk_ref, cbk_ref, L)

    G = g_ref[...] + bias_ref[...]
    bc = _log_sigmoid(G)
    lane = lax.broadcasted_iota(jnp.int32, (2 * H, L), 1)
    s = 1
    while s < L:
        bc = bc + jnp.where(lane >= s, pltpu.roll(bc, s, axis=1), 0.0)
        s *= 2

    row = lax.broadcasted_iota(jnp.int32, (L, L), 0)
    col = lax.broadcasted_iota(jnp.int32, (L, L), 1)
    eye = row == col
    causal = col <= row

    for h in range(H):
        q = q_all[:, h * DK:(h + 1) * DK]
        k = k_all[:, h * DK:(h + 1) * DK]
        v = v_ref[:, h * DV:(h + 1) * DV]
        gi = G[h:h + 1, :]
        b_row = bc[H + h:H + h + 1, :]
        b_col = _row_to_col(b_row, eye)

        m_prev = m_ref[h]
        C = C_ref[h]
        n = n_ref[h]
        D = jnp.where(causal, b_col - b_row + gi, NEG_INF)
        inter = b_col + m_prev
        m_t = jnp.maximum(inter, jnp.max(D, axis=1, keepdims=True))
        g = jnp.exp(inter - m_t)
        qb = q.astype(BF16)
        kb = k.astype(BF16)
        s_qk = _dot_nt(qb, kb) * jnp.exp(D - m_t)
        num = g * _dot(qb, C.astype(BF16)) + _dot(s_qk.astype(BF16), v)
        den = (g * jnp.sum(q * n, axis=1, keepdims=True)
               + jnp.sum(s_qk, axis=1, keepdims=True))
        hh = num / jnp.maximum(jnp.abs(den), jnp.exp(-m_t))

        y = (_rms(hh) * ng_ref[:, h * DV:(h + 1) * DV]
             * jax.nn.sigmoid(o_ref[:, h * DV:(h + 1) * DV].astype(F32)))
        out_ref[:, h * DV:(h + 1) * DV] = y.astype(out_ref.dtype)

        bL = b_row[:, L - 1:L]
        a_row = bL - b_row + gi
        m_new = jnp.maximum(bL + m_prev, jnp.max(a_row, axis=1, keepdims=True))
        ws_col = _row_to_col(jnp.exp(a_row - m_new), eye)
        gC = jnp.exp(bL + m_prev - m_new)
        kw = k * ws_col
        C_ref[h] = gC * C + _dot_tn(kw.astype(BF16), v)
        n_ref[h] = gC * n + jnp.sum(kw, axis=0, keepdims=True)
        m_ref[h] = m_new


def _gla_init(ST_ref, bs_ref):
    ST_ref[...] = jnp.zeros_like(ST_ref)
    bs_ref[...] = jnp.zeros_like(bs_ref)


def _gla_body(q_ref, k_ref, v_ref, gg_ref, sm_ref, w2_ref, ab_ref, ng_ref, out_ref,
              ST_ref, bs_ref, *, L, DK, DV):
    H = HEADS
    z = _dot_hi(sm_ref[...], w2_ref[...]) + ab_ref[...]
    la = _log_sigmoid(z) * G_TAU_INV
    row = lax.broadcasted_iota(jnp.int32, (L, L), 0)
    col = lax.broadcasted_iota(jnp.int32, (L, L), 1)
    tril = jnp.where(col <= row, 1.0, 0.0).astype(F32)
    b_all = _dot_hi(tril, la)
    bs_ref[HALO:HALO + L, :] = b_all
    pos_col = lax.broadcasted_iota(jnp.int32, (L, DK), 0)
    eye = row == col

    for h in range(H):
        dk = slice(h * DK, (h + 1) * DK)
        dv = slice(h * DV, (h + 1) * DV)
        b = b_all[:, dk]
        q = q_ref[:, dk].astype(F32) * (DK ** -0.5)
        k = k_ref[:, dk].astype(F32)
        v = v_ref[:, dv]
        ST = ST_ref[h]
        o = _dot_nt((q * jnp.exp(b)).astype(BF16), ST.astype(BF16))

        A = jnp.where(eye, _dot_nt(q.astype(BF16), k.astype(BF16)), 0.0)
        n = L
        while n >= 2:
            hn = n // 2
            pos = pos_col % n
            if n >= 8:
                R = jnp.concatenate(
                    [jnp.broadcast_to(b[i * n + hn - 1:i * n + hn, :], (n, DK))
                     for i in range(L // n)], axis=0)
            elif n == 4:
                R = jnp.where(pos == 0, bs_ref[HALO + 1:HALO + 1 + L, dk],
                              jnp.where(pos == 1, b,
                                        jnp.where(pos == 2, bs_ref[HALO - 1:HALO - 1 + L, dk],
                                                  bs_ref[HALO - 2:HALO - 2 + L, dk])))
            else:
                R = jnp.where(pos == 0, b, bs_ref[HALO - 1:HALO - 1 + L, dk])
            qn = q * jnp.exp(jnp.where(pos >= hn, b - R, NEG_INF))
            kn = k * jnp.exp(jnp.where(pos < hn, R - b, NEG_INF))
            S = _dot_nt(qn.astype(BF16), kn.astype(BF16))
            A = A + jnp.where((row // n) == (col // n), S, 0.0)
            n = hn
        o = o + _dot(A.astype(BF16), v)

        gg = gg_ref[:, dv].astype(F32)
        y = _rms(o) * ng_ref[:, dv] * (gg * jax.nn.sigmoid(gg))
        out_ref[:, dv] = y.astype(out_ref.dtype)

        bL = b[L - 1:L, :]
        kdec = k * jnp.exp(bL - b)
        ST_ref[h] = ST * jnp.exp(bL) + _dot_tn(v, kdec.astype(BF16))


N_MLSTM_IN, N_GLA_IN, N_MLSTM_SCRATCH = 12, 8, 5


def _scan_kernel(*refs, L, DK, DV):
    m_in = refs[:N_MLSTM_IN]
    g_in = refs[N_MLSTM_IN:N_MLSTM_IN + N_GLA_IN]
    ym_ref, yg_ref = refs[N_MLSTM_IN + N_GLA_IN:N_MLSTM_IN + N_GLA_IN + 2]
    scratch = refs[N_MLSTM_IN + N_GLA_IN + 2:]
    m_scr, g_scr = scratch[:N_MLSTM_SCRATCH], scratch[N_MLSTM_SCRATCH:]

    @pl.when(pl.program_id(1) == 0)
    def _():
        _mlstm_init(*m_scr)
        _gla_init(*g_scr)

    _mlstm_body(*m_in, ym_ref, *m_scr, L=L, DK=DK, DV=DV)
    _gla_body(*g_in, yg_ref, *g_scr, L=L, DK=DK, DV=DV)


def _scans(proj_m, gates, bias, conv_w, conv_b, norm_m, proj_g, small, w2p, a_b, norm_g,
           *, B, S, DK, DV, L):
    H = HEADS
    NC = S // L
    T = B * S
    kern = functools.partial(_scan_kernel, L=L, DK=DK, DV=DV)
    rowblk = lambda b, c: b * NC + c
    t_idx = jnp.arange(L)[None, :, None]
    r_idx = jnp.arange(2 * L)[None, None, :]
    lag = (QK_CONV - 1 - jnp.arange(QK_CONV - 1))[:, None, None]
    shifts = (r_idx == L + t_idx - lag).astype(BF16)
    qk = lambda j: pl.BlockSpec((L, H * DK), lambda b, c: (rowblk(b, c), j))
    vo = lambda j: pl.BlockSpec((L, H * DV), lambda b, c: (rowblk(b, c), j))
    fixed = lambda shape: pl.BlockSpec(shape, lambda b, c: (0,) * len(shape))
    out_spec = pl.BlockSpec((L, H * DV), lambda b, c: (rowblk(b, c), 0))
    out_sds = jax.ShapeDtypeStruct((T, H * DV), BF16)
    return pl.pallas_call(
        kern,
        grid=(B, NC),
        in_specs=[
            fixed((2 * H, 1)),
            qk(0), qk(1), vo(1), vo(2),
            pl.BlockSpec((None, None, 2 * H, L), lambda b, c: (b, c, 0, 0)),
            fixed((QK_CONV - 1, L, 2 * L)),
            pl.BlockSpec((QK_CONV, H * DK), lambda b, c: (0, 0)),
            pl.BlockSpec((QK_CONV, H * DK), lambda b, c: (0, 1)),
            pl.BlockSpec((1, H * DK), lambda b, c: (0, 0)),
            pl.BlockSpec((1, H * DK), lambda b, c: (0, 1)),
            fixed((1, H * DV)),
            qk(0), qk(1), vo(1), vo(2),
            pl.BlockSpec((L, SMALL_W), lambda b, c: (rowblk(b, c), 0)),
            fixed((SMALL_W, H * DK)),
            fixed((1, H * DK)),
            fixed((1, H * DV)),
        ],
        out_specs=[out_spec, out_spec],
        out_shape=[out_sds, out_sds],
        scratch_shapes=[
            pltpu.VMEM((H, DK, DV), F32),
            pltpu.VMEM((H, 1, DK), F32),
            pltpu.VMEM((H, 1, 1), F32),
            pltpu.VMEM((2 * L, H * DK), BF16),
            pltpu.VMEM((2 * L, H * DK), BF16),
            pltpu.VMEM((H, DV, DK), F32),
            pltpu.VMEM((L + 2 * HALO, H * DK), F32),
        ],
        compiler_params=_cparams(("parallel", "arbitrary")),
        name="scans",
    )(bias, proj_m, proj_m, proj_m, proj_m, gates, shifts, conv_w, conv_w, conv_b, conv_b, norm_m,
      proj_g, proj_g, proj_g, proj_g, small, w2p, a_b, norm_g)


def _ffn_up_kernel(h_ref, wg_ref, wu_ref, cw_ref, cb_ref, act_ref, gbuf_ref, ghalo_ref,
                   *, tm, S, tg, groups_last):
    i = pl.program_id(0)
    j = pl.program_id(1)
    nj = pl.num_programs(1)
    ng = act_ref.shape[1] // tg
    seq_start = (i * tm) % S == 0

    @pl.when(jnp.logical_and(i == 0, j == 0))
    def _():
        ghalo_ref[...] = jnp.zeros_like(ghalo_ref)

    def body(n_groups):
        hb = h_ref[...]
        silu = []
        for g in range(n_groups):
            cols = slice(g * tg, (g + 1) * tg)
            gate = _dot(hb, wg_ref[:, cols])
            gbuf_ref[g, 0:HALO, :] = jnp.where(seq_start, 0.0, ghalo_ref[j, :, cols])
            gbuf_ref[g, HALO:HALO + tm, :] = gate
            ghalo_ref[j, :, cols] = gate[tm - HALO:tm, :]
            conv = cb_ref[:, cols] + cw_ref[FFN_CONV - 1:FFN_CONV, cols] * gate
            for t in range(FFN_CONV - 1):
                off = HALO - (FFN_CONV - 1) + t
                conv = conv + cw_ref[t:t + 1, cols] * gbuf_ref[g, off:off + tm, :]
            silu.append(conv * jax.nn.sigmoid(conv))
        for g in range(n_groups):
            cols = slice(g * tg, (g + 1) * tg)
            up = _dot(hb, wu_ref[:, cols])
            act_ref[:, cols] = (silu[g] * up).astype(act_ref.dtype)

    if groups_last == ng:
        body(ng)
    else:
        pl.when(j < nj - 1)(functools.partial(body, ng))
        pl.when(j == nj - 1)(functools.partial(body, groups_last))


def _ffn_up(h2, wg, wu, conv_w, conv_b, *, S):
    T, D = h2.shape
    F = wg.shape[1]
    tm = _pick(S, (1024, 512, 256, 128))
    tg = _pick(F, (MXU_COLS_V7X, 128))
    tn = FFN_TN
    nj = -(-F // tn)
    groups_last = (F - (nj - 1) * tn) // tg
    kern = functools.partial(_ffn_up_kernel, tm=tm, S=S, tg=tg, groups_last=groups_last)
    return pl.pallas_call(
        kern,
        grid=(T // tm, nj),
        in_specs=[
            pl.BlockSpec((tm, D), lambda i, j: (i, 0)),
            pl.BlockSpec((D, tn), lambda i, j: (0, j)),
            pl.BlockSpec((D, tn), lambda i, j: (0, j)),
            pl.BlockSpec((FFN_CONV, tn), lambda i, j: (0, j)),
            pl.BlockSpec((1, tn), lambda i, j: (0, j)),
        ],
        out_specs=pl.BlockSpec((tm, tn), lambda i, j: (i, j)),
        out_shape=jax.ShapeDtypeStruct((T, F), BF16),
        scratch_shapes=[pltpu.VMEM((tn // tg, tm + HALO, tg), F32),
                        pltpu.VMEM((nj, HALO, tn), F32)],
        compiler_params=_cparams(("arbitrary", "arbitrary")),
        name="ffn_up",
    )(h2, wg, wu, conv_w, conv_b)


def _ffn_down_kernel(a_ref, wd_ref, x_ref, gf_ref, out_ref, *, tn, k_tail):
    k = pl.program_id(1)
    nk = pl.num_programs(1)
    tk = a_ref.shape[1]
    D = out_ref.shape[1]

    @pl.when(k == 0)
    def _():
        out_ref[...] = x_ref[...]

    def accumulate(kk):
        a = a_ref[:, :kk]
        for n0 in range(0, D, tn):
            out_ref[:, n0:n0 + tn] += _dot(a, wd_ref[:kk, n0:n0 + tn])

    if k_tail == tk:
        accumulate(tk)
    else:
        pl.when(k < nk - 1)(lambda: accumulate(tk))
        pl.when(k == nk - 1)(lambda: accumulate(k_tail))

    @pl.when(k == nk - 1)
    def _():
        out_ref[...] = _rms(out_ref[...]) * gf_ref[...]


def _ffn_down(act, wd, x1, lnf_g):
    T, F = act.shape
    D = wd.shape[1]
    tm = _pick(T, (512, 256, 128))
    tn = _pick(D, (1024, 512, 256, 128))
    tk = FFN_TK
    nk = -(-F // tk)
    kern = functools.partial(_ffn_down_kernel, tn=tn, k_tail=F - (nk - 1) * tk)
    return pl.pallas_call(
        kern,
        grid=(T // tm, nk),
        in_specs=[
            pl.BlockSpec((tm, tk), lambda i, k: (i, k)),
            pl.BlockSpec((tk, D), lambda i, k: (k, 0)),
            pl.BlockSpec((tm, D), lambda i, k: (i, 0)),
            pl.BlockSpec((1, D), lambda i, k: (0, 0)),
        ],
        out_specs=pl.BlockSpec((tm, D), lambda i, k: (i, 0)),
        out_shape=jax.ShapeDtypeStruct((T, D), F32),
        compiler_params=_cparams(("parallel", "arbitrary")),
        name="ffn_down",
    )(act, wd, x1, lnf_g)


def kernel(x, ln1_g, w_in, mlstm_conv_w, mlstm_conv_b, mlstm_i_b, mlstm_f_b, mlstm_norm_g,
           gla_a_w2, gla_a_b, gla_norm_g, w_out, ln2_g, w_ffn_gate, w_ffn_up,
           ffn_conv_w, ffn_conv_b, w_ffn_down, lnf_g):
    B, S, D = x.shape
    T = B * S
    H = HEADS
    M_DK = mlstm_conv_w.shape[1] // (2 * H)
    M_DV = mlstm_norm_g.shape[0] // H
    G_DK = gla_a_w2.shape[1] // H
    G_DV = gla_norm_g.shape[0] // H
    F = w_ffn_gate.shape[1]
    assert M_DV == 2 * M_DK and G_DV == 2 * G_DK and M_DK == G_DK

    n_m = 2 * H * M_DK + 2 * H * M_DV
    n_g = 2 * H * G_DK + 2 * H * G_DV
    o_gate = n_m
    o_g = n_m + 2 * H
    o_a = o_g + n_g
    assert w_in.shape[1] == o_a + G_RANK

    w_in_t = w_in.T.astype(BF16)
    w_g = w_in_t[o_g:o_a]
    w_small = jnp.concatenate(
        [w_in_t[o_gate:o_g], w_in_t[o_a:],
         jnp.zeros((SMALL_W - 2 * H - G_RANK, D), BF16)], axis=0)

    x2d = x.reshape(T, D)
    h1 = _rmsnorm(x2d, ln1_g, BF16)
    proj_m = _matmul_nt(h1, w_in_t, BF16, "inproj_mlstm", n_out=n_m)
    proj_g = _matmul_nt(h1, w_g, BF16, "inproj_gla")
    small = _matmul_nt(h1, w_small, F32, "inproj_small")

    L = 128
    gates = small[:, :2 * H].reshape(B, S // L, L, 2 * H)
    gates = jnp.transpose(gates, (0, 1, 3, 2))
    bias = jnp.concatenate([mlstm_i_b, mlstm_f_b]).astype(F32).reshape(2 * H, 1)
    w2p = jnp.zeros((SMALL_W, H * G_DK), F32).at[2 * H:2 * H + G_RANK].set(gla_a_w2.astype(F32))
    ym, yg = _scans(proj_m, gates, bias, mlstm_conv_w.astype(F32),
                    mlstm_conv_b.reshape(1, -1).astype(F32), mlstm_norm_g.reshape(1, -1).astype(F32),
                    proj_g, small, w2p, gla_a_b.reshape(1, -1).astype(F32),
                    gla_norm_g.reshape(1, -1).astype(F32), B=B, S=S, DK=M_DK, DV=M_DV, L=L)

    x1 = _outproj(ym, yg, w_out.astype(BF16), x2d)

    h2 = _rmsnorm(x1, ln2_g, BF16)
    act = _ffn_up(h2, w_ffn_gate.astype(BF16), w_ffn_up.astype(BF16), ffn_conv_w.astype(F32),
                  ffn_conv_b.reshape(1, -1).astype(F32), S=S)
    out = _ffn_down(act, w_ffn_down.astype(BF16), x1, lnf_g.reshape(1, -1).astype(F32))
    return out.reshape(B, S, D)
```

```python
import functools

import jax
import jax.numpy as jnp
from jax import lax
from jax.experimental import pallas as pl
from jax.experimental.pallas import tpu as pltpu

F32 = jnp.float32
BF16 = jnp.bfloat16

HEADS = 4
QK_CONV = 4
FFN_CONV = 3
G_RANK = 16
G_TAU_INV = 1.0 / 16.0
EPS = 1e-6
SMALL_W = 128
HALO = 8
NEG_INF = float("-inf")
FFN_TN = 512
FFN_TK = 1024
MXU_COLS_V7X = 256
LANES = 128

VMEM_LIMIT_V7X = 56 * 1024 * 1024


def _cparams(sem, vmem=VMEM_LIMIT_V7X):
    return pltpu.CompilerParams(dimension_semantics=sem, vmem_limit_bytes=vmem)


def _pick(n, prefs):
    for p in prefs:
        if n % p == 0:
            return p
    raise ValueError(f"no tile in {prefs} divides {n}")


def _log_sigmoid(x):
    return jnp.minimum(x, 0.0) - jnp.log1p(jnp.exp(-jnp.abs(x)))


def _dot(a, b):
    return jnp.dot(a, b, preferred_element_type=F32)


def _dot_hi(a, b):
    return jnp.dot(a, b, preferred_element_type=F32, precision=lax.Precision.HIGHEST)


def _dot_nt(a, b):
    return lax.dot_general(a, b, (((1,), (1,)), ((), ())), preferred_element_type=F32)


def _dot_tn(a, b):
    return lax.dot_general(a, b, (((0,), (0,)), ((), ())), preferred_element_type=F32)


def _rms(x):
    return x * lax.rsqrt(jnp.mean(x * x, axis=-1, keepdims=True) + EPS)


def _rmsnorm_kernel(x_ref, g_ref, o_ref):
    o_ref[...] = (_rms(x_ref[...].astype(F32)) * g_ref[...]).astype(o_ref.dtype)


def _rmsnorm(x, g, out_dtype):
    T, D = x.shape
    tr = _pick(T, (512, 256, 128, 64, 8))
    return pl.pallas_call(
        _rmsnorm_kernel,
        grid=(T // tr,),
        in_specs=[pl.BlockSpec((tr, D), lambda i: (i, 0)),
                  pl.BlockSpec((1, D), lambda i: (0, 0))],
        out_specs=pl.BlockSpec((tr, D), lambda i: (i, 0)),
        out_shape=jax.ShapeDtypeStruct((T, D), out_dtype),
        compiler_params=_cparams(("parallel",)),
        name="rmsnorm",
    )(x, g.reshape(1, D).astype(F32))


def _cast_block(n, steps, min_block):
    cands = [c for c in range(LANES, n + 1, LANES) if n % c == 0 and n // c <= steps]
    wide = [c for c in cands if c >= min_block]
    return (wide or cands)[0]


def _matmul_nt_kernel(a_ref, wt_ref, *rest, n_cast):
    o_ref = rest[-2] if n_cast else rest[-1]
    o_ref[...] = _dot_nt(a_ref[...], wt_ref[...]).astype(o_ref.dtype)
    if n_cast:
        src_ref, dst_ref = rest[0], rest[-1]
        step = pl.program_id(0) * pl.num_programs(1) + pl.program_id(1)

        @pl.when(step < n_cast)
        def _():
            dst_ref[...] = src_ref[...].astype(dst_ref.dtype)


def _matmul_nt(a, wt, out_dtype, name, n_out=None, cast_src=None):
    M, K = a.shape
    N = wt.shape[0] if n_out is None else n_out
    tm = _pick(M, (1024, 512, 256, 128))
    tn = _pick(N, (1024, 512, 256, 128))
    grid = (M // tm, N // tn)
    in_specs = [pl.BlockSpec((tm, K), lambda i, j: (i, 0)),
                pl.BlockSpec((tn, K), lambda i, j: (j, 0))]
    out_specs = [pl.BlockSpec((tm, tn), lambda i, j: (i, j))]
    out_shape = [jax.ShapeDtypeStruct((M, N), out_dtype)]
    args = [a, wt]
    n_cast = 0
    if cast_src is not None:
        R, C = cast_src.shape
        cb = _cast_block(C, grid[0] * grid[1], 2 * LANES)
        n_cast = C // cb
        blk = pl.BlockSpec((R, cb), lambda i, j: (0, jnp.minimum(i * grid[1] + j, n_cast - 1)))
        in_specs.append(blk)
        out_specs.append(blk)
        out_shape.append(jax.ShapeDtypeStruct((R, C), BF16))
        args.append(cast_src)
    res = pl.pallas_call(
        functools.partial(_matmul_nt_kernel, n_cast=n_cast),
        grid=grid,
        in_specs=in_specs,
        out_specs=out_specs,
        out_shape=out_shape,
        compiler_params=_cparams(("arbitrary", "arbitrary")),
        name=name,
    )(*args)
    return res if n_cast else res[0]


def _outproj_kernel(ym_ref, yg_ref, wt_ref, wb_ref, x_ref, o_ref):
    acc = _dot(ym_ref[...], wt_ref[...]) + _dot(yg_ref[...], wb_ref[...])
    o_ref[...] = x_ref[...] + acc


def _outproj(ym, yg, w, x):
    M, K2 = ym.shape
    N = w.shape[1]
    tm = _pick(M, (1024, 512, 256, 128))
    tn = _pick(N, (1024, 512, 256, 128))
    return pl.pallas_call(
        _outproj_kernel,
        grid=(M // tm, N // tn),
        in_specs=[pl.BlockSpec((tm, K2), lambda i, j: (i, 0)),
                  pl.BlockSpec((tm, K2), lambda i, j: (i, 0)),
                  pl.BlockSpec((K2, tn), lambda i, j: (0, j)),
                  pl.BlockSpec((K2, tn), lambda i, j: (1, j)),
                  pl.BlockSpec((tm, tn), lambda i, j: (i, j))],
        out_specs=pl.BlockSpec((tm, tn), lambda i, j: (i, j)),
        out_shape=jax.ShapeDtypeStruct((M, N), F32),
        compiler_params=_cparams(("parallel", "arbitrary")),
        name="outproj",
    )(ym, yg, w, w, x)


def _conv_silu(raw_ref, x_ref, shift_ref, w_ref, b_ref, L):
    raw = raw_ref[...]
    x_ref[L:2 * L, :] = raw
    acc = b_ref[...] + w_ref[QK_CONV - 1:QK_CONV, :] * raw.astype(F32)
    for j in range(QK_CONV - 1):
        acc = acc + w_ref[j:j + 1, :] * _dot(shift_ref[j], x_ref[...])
    x_ref[0:L, :] = raw
    return acc * jax.nn.sigmoid(acc)


def _row_to_col(row, eye):
    L = eye.shape[0]
    return jnp.sum(jnp.where(eye, jnp.broadcast_to(row, (L, L)), 0.0), axis=1, keepdims=True)


def _mlstm_init(C_ref, n_ref, m_ref, xq_ref, xk_ref):
    L = xq_ref.shape[0] // 2
    C_ref[...] = jnp.zeros_like(C_ref)
    n_ref[...] = jnp.zeros_like(n_ref)
    m_ref[...] = jnp.zeros_like(m_ref)
    xq_ref[0:L, :] = jnp.zeros((L, xq_ref.shape[1]), BF16)
    xk_ref[0:L, :] = jnp.zeros((L, xk_ref.shape[1]), BF16)


def _mlstm_body(bias_ref, q_ref, k_ref, v_ref, o_ref, g_ref, sh_ref, cwq_ref, cwk_ref, cbq_ref,
                cbk_ref, ng_ref, out_ref, C_ref, n_ref, m_ref, xq_ref, xk_ref, *, L, DK, DV):
    H = HEADS
    q_all = _conv_silu(q_ref, xq_ref, sh_ref, cwq_ref, cbq_ref, L) * (DK ** -0.5)
    k_all = _conv_silu(k_ref, xk_ref, sh_ref, cwk_ref, cbk_ref, L)

    G = g_ref[...] + bias_ref[...]
    bc = _log_sigmoid(G)
    lane = lax.broadcasted_iota(jnp.int32, (2 * H, L), 1)
    s = 1
    while s < L:
        bc = bc + jnp.where(lane >= s, pltpu.roll(bc, s, axis=1), 0.0)
        s *= 2

    row = lax.broadcasted_iota(jnp.int32, (L, L), 0)
    col = lax.broadcasted_iota(jnp.int32, (L, L), 1)
    eye = row == col
    causal = col <= row

    for h in range(H):
        q = q_all[:, h * DK:(h + 1) * DK]
        k = k_all[:, h * DK:(h + 1) * DK]
        v = v_ref[:, h * DV:(h + 1) * DV]
        gi = G[h:h + 1, :]
        b_row = bc[H + h:H + h + 1, :]
        b_col = _row_to_col(b_row, eye)

        m_prev = m_ref[h]
        C = C_ref[h]
        n = n_ref[h]
        D = jnp.where(causal, b_col - b_row + gi, NEG_INF)
        inter = b_col + m_prev
        m_t = jnp.maximum(inter, jnp.max(D, axis=1, keepdims=True))
        g = jnp.exp(inter - m_t)
        qb = q.astype(BF16)
        kb = k.astype(BF16)
        s_qk = _dot_nt(qb, kb) * jnp.exp(D - m_t)
        num = g * _dot(qb, C.astype(BF16)) + _dot(s_qk.astype(BF16), v)
        den = (g * jnp.sum(q * n, axis=1, keepdims=True)
               + jnp.sum(s_qk, axis=1, keepdims=True))
        hh = num / jnp.maximum(jnp.abs(den), jnp.exp(-m_t))

        y = (_rms(hh) * ng_ref[:, h * DV:(h + 1) * DV]
             * jax.nn.sigmoid(o_ref[:, h * DV:(h + 1) * DV].astype(F32)))
        out_ref[:, h * DV:(h + 1) * DV] = y.astype(out_ref.dtype)

        bL = b_row[:, L - 1:L]
        a_row = bL - b_row + gi
        m_new = jnp.maximum(bL + m_prev, jnp.max(a_row, axis=1, keepdims=True))
        ws_col = _row_to_col(jnp.exp(a_row - m_new), eye)
        gC = jnp.exp(bL + m_prev - m_new)
        kw = k * ws_col
        C_ref[h] = gC * C + _dot_tn(kw.astype(BF16), v)
        n_ref[h] = gC * n + jnp.sum(kw, axis=0, keepdims=True)
        m_ref[h] = m_new


def _gla_init(ST_ref, bs_ref):
    ST_ref[...] = jnp.zeros_like(ST_ref)
    bs_ref[...] = jnp.zeros_like(bs_ref)


def _gla_body(q_ref, k_ref, v_ref, gg_ref, sm_ref, w2_ref, ab_ref, ng_ref, out_ref,
              ST_ref, bs_ref, *, L, DK, DV):
    H = HEADS
    z = _dot_hi(sm_ref[...], w2_ref[...]) + ab_ref[...]
    la = _log_sigmoid(z) * G_TAU_INV
    row = lax.broadcasted_iota(jnp.int32, (L, L), 0)
    col = lax.broadcasted_iota(jnp.int32, (L, L), 1)
    tril = jnp.where(col <= row, 1.0, 0.0).astype(F32)
    b_all = _dot_hi(tril, la)
    bs_ref[HALO:HALO + L, :] = b_all
    pos_col = lax.broadcasted_iota(jnp.int32, (L, DK), 0)
    eye = row == col

    for h in range(H):
        dk = slice(h * DK, (h + 1) * DK)
        dv = slice(h * DV, (h + 1) * DV)
        b = b_all[:, dk]
        q = q_ref[:, dk].astype(F32) * (DK ** -0.5)
        k = k_ref[:, dk].astype(F32)
        v = v_ref[:, dv]
        ST = ST_ref[h]
        o = _dot_nt((q * jnp.exp(b)).astype(BF16), ST.astype(BF16))

        A = jnp.where(eye, _dot_nt(q.astype(BF16), k.astype(BF16)), 0.0)
        n = L
        while n >= 2:
            hn = n // 2
            pos = pos_col % n
            if n >= 8:
                R = jnp.concatenate(
                    [jnp.broadcast_to(b[i * n + hn - 1:i * n + hn, :], (n, DK))
                     for i in range(L // n)], axis=0)
            elif n == 4:
                R = jnp.where(pos == 0, bs_ref[HALO + 1:HALO + 1 + L, dk],
                              jnp.where(pos == 1, b,
                                        jnp.where(pos == 2, bs_ref[HALO - 1:HALO - 1 + L, dk],
                                                  bs_ref[HALO - 2:HALO - 2 + L, dk])))
            else:
                R = jnp.where(pos == 0, b, bs_ref[HALO - 1:HALO - 1 + L, dk])
            qn = q * jnp.exp(jnp.where(pos >= hn, b - R, NEG_INF))
            kn = k * jnp.exp(jnp.where(pos < hn, R - b, NEG_INF))
            S = _dot_nt(qn.astype(BF16), kn.astype(BF16))
            A = A + jnp.where((row // n) == (col // n), S, 0.0)
            n = hn
        o = o + _dot(A.astype(BF16), v)

        gg = gg_ref[:, dv].astype(F32)
        y = _rms(o) * ng_ref[:, dv] * (gg * jax.nn.sigmoid(gg))
        out_ref[:, dv] = y.astype(out_ref.dtype)

        bL = b[L - 1:L, :]
        kdec = k * jnp.exp(bL - b)
        ST_ref[h] = ST * jnp.exp(bL) + _dot_tn(v, kdec.astype(BF16))


N_MLSTM_IN, N_GLA_IN, N_MLSTM_SCRATCH = 12, 8, 5


def _scan_kernel(*refs, L, DK, DV):
    m_in = refs[:N_MLSTM_IN]
    g_in = refs[N_MLSTM_IN:N_MLSTM_IN + N_GLA_IN]
    ym_ref, yg_ref = refs[N_MLSTM_IN + N_GLA_IN:N_MLSTM_IN + N_GLA_IN + 2]
    scratch = refs[N_MLSTM_IN + N_GLA_IN + 2:]
    m_scr, g_scr = scratch[:N_MLSTM_SCRATCH], scratch[N_MLSTM_SCRATCH:]

    @pl.when(pl.program_id(1) == 0)
    def _():
        _mlstm_init(*m_scr)
        _gla_init(*g_scr)

    _mlstm_body(*m_in, ym_ref, *m_scr, L=L, DK=DK, DV=DV)
    _gla_body(*g_in, yg_ref, *g_scr, L=L, DK=DK, DV=DV)


def _scans(proj_m, gates, bias, conv_w, conv_b, norm_m, proj_g, small, w2p, a_b, norm_g,
           *, B, S, DK, DV, L):
    H = HEADS
    NC = S // L
    T = B * S
    kern = functools.partial(_scan_kernel, L=L, DK=DK, DV=DV)
    rowblk = lambda b, c: b * NC + c
    t_idx = jnp.arange(L)[None, :, None]
    r_idx = jnp.arange(2 * L)[None, None, :]
    lag = (QK_CONV - 1 - jnp.arange(QK_CONV - 1))[:, None, None]
    shifts = (r_idx == L + t_idx - lag).astype(BF16)
    qk = lambda j: pl.BlockSpec((L, H * DK), lambda b, c: (rowblk(b, c), j))
    vo = lambda j: pl.BlockSpec((L, H * DV), lambda b, c: (rowblk(b, c), j))
    fixed = lambda shape: pl.BlockSpec(shape, lambda b, c: (0,) * len(shape))
    out_spec = pl.BlockSpec((L, H * DV), lambda b, c: (rowblk(b, c), 0))
    out_sds = jax.ShapeDtypeStruct((T, H * DV), BF16)
    return pl.pallas_call(
        kern,
        grid=(B, NC),
        in_specs=[
            fixed((2 * H, 1)),
            qk(0), qk(1), vo(1), vo(2),
            pl.BlockSpec((None, None, 2 * H, L), lambda b, c: (b, c, 0, 0)),
            fixed((QK_CONV - 1, L, 2 * L)),
            pl.BlockSpec((QK_CONV, H * DK), lambda b, c: (0, 0)),
            pl.BlockSpec((QK_CONV, H * DK), lambda b, c: (0, 1)),
            pl.BlockSpec((1, H * DK), lambda b, c: (0, 0)),
            pl.BlockSpec((1, H * DK), lambda b, c: (0, 1)),
            fixed((1, H * DV)),
            qk(0), qk(1), vo(1), vo(2),
            pl.BlockSpec((L, SMALL_W), lambda b, c: (rowblk(b, c), 0)),
            fixed((SMALL_W, H * DK)),
            fixed((1, H * DK)),
            fixed((1, H * DV)),
        ],
        out_specs=[out_spec, out_spec],
        out_shape=[out_sds, out_sds],
        scratch_shapes=[
            pltpu.VMEM((H, DK, DV), F32),
            pltpu.VMEM((H, 1, DK), F32),
            pltpu.VMEM((H, 1, 1), F32),
            pltpu.VMEM((2 * L, H * DK), BF16),
            pltpu.VMEM((2 * L, H * DK), BF16),
            pltpu.VMEM((H, DV, DK), F32),
            pltpu.VMEM((L + 2 * HALO, H * DK), F32),
        ],
        compiler_params=_cparams(("parallel", "arbitrary")),
        name="scans",
    )(bias, proj_m, proj_m, proj_m, proj_m, gates, shifts, conv_w, conv_w, conv_b, conv_b, norm_m,
      proj_g, proj_g, proj_g, proj_g, small, w2p, a_b, norm_g)


def _ffn_up_kernel(h_ref, wg_ref, wu_ref, cw_ref, cb_ref, wd_ref, act_ref, wdb_ref, gbuf_ref, ghalo_ref,
                   *, tm, S, tg, groups_last, n_cast):
    i = pl.program_id(0)
    j = pl.program_id(1)
    nj = pl.num_programs(1)
    ng = act_ref.shape[1] // tg
    seq_start = (i * tm) % S == 0

    @pl.when(jnp.logical_and(i == 0, j == 0))
    def _():
        ghalo_ref[...] = jnp.zeros_like(ghalo_ref)

    @pl.when(i * nj + j < n_cast)
    def _():
        wdb_ref[...] = wd_ref[...].astype(wdb_ref.dtype)

    def body(n_groups):
        hb = h_ref[...]
        silu = []
        for g in range(n_groups):
            cols = slice(g * tg, (g + 1) * tg)
            gate = _dot(hb, wg_ref[:, cols])
            gbuf_ref[g, 0:HALO, :] = jnp.where(seq_start, 0.0, ghalo_ref[j, :, cols])
            gbuf_ref[g, HALO:HALO + tm, :] = gate
            ghalo_ref[j, :, cols] = gate[tm - HALO:tm, :]
            conv = cb_ref[:, cols] + cw_ref[FFN_CONV - 1:FFN_CONV, cols] * gate
            for t in range(FFN_CONV - 1):
                off = HALO - (FFN_CONV - 1) + t
                conv = conv + cw_ref[t:t + 1, cols] * gbuf_ref[g, off:off + tm, :]
            silu.append(conv * jax.nn.sigmoid(conv))
        for g in range(n_groups):
            cols = slice(g * tg, (g + 1) * tg)
            up = _dot(hb, wu_ref[:, cols])
            act_ref[:, cols] = (silu[g] * up).astype(act_ref.dtype)

    if groups_last == ng:
        body(ng)
    else:
        pl.when(j < nj - 1)(functools.partial(body, ng))
        pl.when(j == nj - 1)(functools.partial(body, groups_last))


def _ffn_up(h2, wg, wu, conv_w, conv_b, wd, *, S):
    T, D = h2.shape
    F = wg.shape[1]
    tm = _pick(S, (1024, 512, 256, 128))
    tg = _pick(F, (MXU_COLS_V7X, 128))
    tn = FFN_TN
    nj = -(-F // tn)
    groups_last = (F - (nj - 1) * tn) // tg
    rb = _cast_block(F, (T // tm) * nj, LANES)
    n_cast = F // rb
    kern = functools.partial(_ffn_up_kernel, tm=tm, S=S, tg=tg, groups_last=groups_last,
                             n_cast=n_cast)
    wd_blk = pl.BlockSpec((rb, D), lambda i, j: (jnp.minimum(i * nj + j, n_cast - 1), 0))
    return pl.pallas_call(
        kern,
        grid=(T // tm, nj),
        in_specs=[
            pl.BlockSpec((tm, D), lambda i, j: (i, 0)),
            pl.BlockSpec((D, tn), lambda i, j: (0, j)),
            pl.BlockSpec((D, tn), lambda i, j: (0, j)),
            pl.BlockSpec((FFN_CONV, tn), lambda i, j: (0, j)),
            pl.BlockSpec((1, tn), lambda i, j: (0, j)),
            wd_blk,
        ],
        out_specs=[pl.BlockSpec((tm, tn), lambda i, j: (i, j)), wd_blk],
        out_shape=[jax.ShapeDtypeStruct((T, F), BF16), jax.ShapeDtypeStruct((F, D), BF16)],
        scratch_shapes=[pltpu.VMEM((tn // tg, tm + HALO, tg), F32),
                        pltpu.VMEM((nj, HALO, tn), F32)],
        compiler_params=_cparams(("arbitrary", "arbitrary")),
        name="ffn_up",
    )(h2, wg, wu, conv_w, conv_b, wd)


def _ffn_down_kernel(a_ref, wd_ref, x_ref, gf_ref, out_ref, *, tn, k_tail):
    k = pl.program_id(1)
    nk = pl.num_programs(1)
    tk = a_ref.shape[1]
    D = out_ref.shape[1]

    @pl.when(k == 0)
    def _():
        out_ref[...] = x_ref[...]

    def accumulate(kk):
        a = a_ref[:, :kk]
        for n0 in range(0, D, tn):
            out_ref[:, n0:n0 + tn] += _dot(a, wd_ref[:kk, n0:n0 + tn])

    if k_tail == tk:
        accumulate(tk)
    else:
        pl.when(k < nk - 1)(lambda: accumulate(tk))
        pl.when(k == nk - 1)(lambda: accumulate(k_tail))

    @pl.when(k == nk - 1)
    def _():
        out_ref[...] = _rms(out_ref[...]) * gf_ref[...]


def _ffn_down(act, wd, x1, lnf_g):
    T, F = act.shape
    D = wd.shape[1]
    tm = _pick(T, (512, 256, 128))
    tn = _pick(D, (1024, 512, 256, 128))
    tk = FFN_TK
    nk = -(-F // tk)
    kern = functools.partial(_ffn_down_kernel, tn=tn, k_tail=F - (nk - 1) * tk)
    return pl.pallas_call(
        kern,
        grid=(T // tm, nk),
        in_specs=[
            pl.BlockSpec((tm, tk), lambda i, k: (i, k)),
            pl.BlockSpec((tk, D), lambda i, k: (k, 0)),
            pl.BlockSpec((tm, D), lambda i, k: (i, 0)),
            pl.BlockSpec((1, D), lambda i, k: (0, 0)),
        ],
        out_specs=pl.BlockSpec((tm, D), lambda i, k: (i, 0)),
        out_shape=jax.ShapeDtypeStruct((T, D), F32),
        compiler_params=_cparams(("parallel", "arbitrary")),
        name="ffn_down",
    )(act, wd, x1, lnf_g)


def kernel(x, ln1_g, w_in, mlstm_conv_w, mlstm_conv_b, mlstm_i_b, mlstm_f_b, mlstm_norm_g,
           gla_a_w2, gla_a_b, gla_norm_g, w_out, ln2_g, w_ffn_gate, w_ffn_up,
           ffn_conv_w, ffn_conv_b, w_ffn_down, lnf_g):
    B, S, D = x.shape
    T = B * S
    H = HEADS
    M_DK = mlstm_conv_w.shape[1] // (2 * H)
    M_DV = mlstm_norm_g.shape[0] // H
    G_DK = gla_a_w2.shape[1] // H
    G_DV = gla_norm_g.shape[0] // H
    F = w_ffn_gate.shape[1]
    assert M_DV == 2 * M_DK and G_DV == 2 * G_DK and M_DK == G_DK

    n_m = 2 * H * M_DK + 2 * H * M_DV
    n_g = 2 * H * G_DK + 2 * H * G_DV
    o_gate = n_m
    o_g = n_m + 2 * H
    o_a = o_g + n_g
    assert w_in.shape[1] == o_a + G_RANK

    w_in_t = w_in.T.astype(BF16)
    w_g = w_in_t[o_g:o_a]
    w_small = jnp.concatenate(
        [w_in_t[o_gate:o_g], w_in_t[o_a:],
         jnp.zeros((SMALL_W - 2 * H - G_RANK, D), BF16)], axis=0)

    x2d = x.reshape(T, D)
    h1 = _rmsnorm(x2d, ln1_g, BF16)
    proj_m, wg_b = _matmul_nt(h1, w_in_t, BF16, "inproj_mlstm", n_out=n_m,
                              cast_src=w_ffn_gate)
    proj_g, wu_b = _matmul_nt(h1, w_g, BF16, "inproj_gla", cast_src=w_ffn_up)
    small = _matmul_nt(h1, w_small, F32, "inproj_small")

    L = 128
    gates = small[:, :2 * H].reshape(B, S // L, L, 2 * H)
    gates = jnp.transpose(gates, (0, 1, 3, 2))
    bias = jnp.concatenate([mlstm_i_b, mlstm_f_b]).astype(F32).reshape(2 * H, 1)
    w2p = jnp.zeros((SMALL_W, H * G_DK), F32).at[2 * H:2 * H + G_RANK].set(gla_a_w2.astype(F32))
    ym, yg = _scans(proj_m, gates, bias, mlstm_conv_w.astype(F32),
                    mlstm_conv_b.reshape(1, -1).astype(F32), mlstm_norm_g.reshape(1, -1).astype(F32),
                    proj_g, small, w2p, gla_a_b.reshape(1, -1).astype(F32),
                    gla_norm_g.reshape(1, -1).astype(F32), B=B, S=S, DK=M_DK, DV=M_DV, L=L)

    x1 = _outproj(ym, yg, w_out.astype(BF16), x2d)

    h2 = _rmsnorm(x1, ln2_g, BF16)
    act, wd_b = _ffn_up(h2, wg_b, wu_b, ffn_conv_w.astype(F32),
                        ffn_conv_b.reshape(1, -1).astype(F32), w_ffn_down, S=S)
    out = _ffn_down(act, wd_b, x1, lnf_g.reshape(1, -1).astype(F32))
    return out.reshape(B, S, D)
```

```python
import functools

import jax
import jax.numpy as jnp
from jax import lax
from jax.experimental import pallas as pl
from jax.experimental.pallas import tpu as pltpu

F32 = jnp.float32
BF16 = jnp.bfloat16

HEADS = 4
QK_CONV = 4
FFN_CONV = 3
G_RANK = 16
G_TAU_INV = 1.0 / 16.0
EPS = 1e-6
SMALL_W = 128
HALO = 8
NEG_INF = float("-inf")
FFN_TN = 512
FFN_TK = 1024
MXU_COLS_V7X = 256
LANES = 128

VMEM_LIMIT_V7X = 56 * 1024 * 1024


def _cparams(sem, vmem=VMEM_LIMIT_V7X):
    return pltpu.CompilerParams(dimension_semantics=sem, vmem_limit_bytes=vmem)


def _pick(n, prefs):
    for p in prefs:
        if n % p == 0:
            return p
    raise ValueError(f"no tile in {prefs} divides {n}")


def _log_sigmoid(x):
    return jnp.minimum(x, 0.0) - jnp.log1p(jnp.exp(-jnp.abs(x)))


def _dot(a, b):
    return jnp.dot(a, b, preferred_element_type=F32)


def _dot_hi(a, b):
    return jnp.dot(a, b, preferred_element_type=F32, precision=lax.Precision.HIGHEST)


def _dot_nt(a, b):
    return lax.dot_general(a, b, (((1,), (1,)), ((), ())), preferred_element_type=F32)


def _dot_tn(a, b):
    return lax.dot_general(a, b, (((0,), (0,)), ((), ())), preferred_element_type=F32)


def _rms(x):
    return x * lax.rsqrt(jnp.mean(x * x, axis=-1, keepdims=True) + EPS)


def _rmsnorm_kernel(x_ref, g_ref, o_ref):
    o_ref[...] = (_rms(x_ref[...].astype(F32)) * g_ref[...]).astype(o_ref.dtype)


def _rmsnorm(x, g, out_dtype):
    T, D = x.shape
    tr = _pick(T, (512, 256, 128, 64, 8))
    return pl.pallas_call(
        _rmsnorm_kernel,
        grid=(T // tr,),
        in_specs=[pl.BlockSpec((tr, D), lambda i: (i, 0)),
                  pl.BlockSpec((1, D), lambda i: (0, 0))],
        out_specs=pl.BlockSpec((tr, D), lambda i: (i, 0)),
        out_shape=jax.ShapeDtypeStruct((T, D), out_dtype),
        compiler_params=_cparams(("parallel",)),
        name="rmsnorm",
    )(x, g.reshape(1, D).astype(F32))


def _cast_block(n, steps, min_block):
    cands = [c for c in range(LANES, n + 1, LANES) if n % c == 0 and n // c <= steps]
    wide = [c for c in cands if c >= min_block]
    return (wide or cands)[0]


def _matmul_nt_kernel(a_ref, wt_ref, *rest, n_cast):
    o_ref = rest[-2] if n_cast else rest[-1]
    o_ref[...] = _dot_nt(a_ref[...], wt_ref[...]).astype(o_ref.dtype)
    if n_cast:
        src_ref, dst_ref = rest[0], rest[-1]
        step = pl.program_id(0) * pl.num_programs(1) + pl.program_id(1)

        @pl.when(step < n_cast)
        def _():
            dst_ref[...] = src_ref[...].astype(dst_ref.dtype)


def _matmul_nt(a, wt, out_dtype, name, n_out=None, cast_src=None):
    M, K = a.shape
    N = wt.shape[0] if n_out is None else n_out
    tm = _pick(M, (1024, 512, 256, 128))
    tn = _pick(N, (1024, 512, 256, 128))
    grid = (M // tm, N // tn)
    in_specs = [pl.BlockSpec((tm, K), lambda i, j: (i, 0)),
                pl.BlockSpec((tn, K), lambda i, j: (j, 0))]
    out_specs = [pl.BlockSpec((tm, tn), lambda i, j: (i, j))]
    out_shape = [jax.ShapeDtypeStruct((M, N), out_dtype)]
    args = [a, wt]
    n_cast = 0
    if cast_src is not None:
        R, C = cast_src.shape
        cb = _cast_block(C, grid[0] * grid[1], 2 * LANES)
        n_cast = C // cb
        blk = pl.BlockSpec((R, cb), lambda i, j: (0, jnp.minimum(i * grid[1] + j, n_cast - 1)))
        in_specs.append(blk)
        out_specs.append(blk)
        out_shape.append(jax.ShapeDtypeStruct((R, C), BF16))
        args.append(cast_src)
    res = pl.pallas_call(
        functools.partial(_matmul_nt_kernel, n_cast=n_cast),
        grid=grid,
        in_specs=in_specs,
        out_specs=out_specs,
        out_shape=out_shape,
        compiler_params=_cparams(("arbitrary", "arbitrary")),
        name=name,
    )(*args)
    return res if n_cast else res[0]


def _outproj_kernel(ym_ref, yg_ref, wt_ref, wb_ref, x_ref, o_ref):
    acc = _dot(ym_ref[...], wt_ref[...]) + _dot(yg_ref[...], wb_ref[...])
    o_ref[...] = x_ref[...] + acc


def _outproj(ym, yg, w, x):
    M, K2 = ym.shape
    N = w.shape[1]
    tm = _pick(M, (1024, 512, 256, 128))
    tn = _pick(N, (1024, 512, 256, 128))
    return pl.pallas_call(
        _outproj_kernel,
        grid=(M // tm, N // tn),
        in_specs=[pl.BlockSpec((tm, K2), lambda i, j: (i, 0)),
                  pl.BlockSpec((tm, K2), lambda i, j: (i, 0)),
                  pl.BlockSpec((K2, tn), lambda i, j: (0, j)),
                  pl.BlockSpec((K2, tn), lambda i, j: (1, j)),
                  pl.BlockSpec((tm, tn), lambda i, j: (i, j))],
        out_specs=pl.BlockSpec((tm, tn), lambda i, j: (i, j)),
        out_shape=jax.ShapeDtypeStruct((M, N), F32),
        compiler_params=_cparams(("parallel", "arbitrary")),
        name="outproj",
    )(ym, yg, w, w, x)


def _conv_silu(raw_ref, x_ref, shift_ref, w_ref, b_ref, L):
    raw = raw_ref[...]
    x_ref[L:2 * L, :] = raw
    acc = b_ref[...] + w_ref[QK_CONV - 1:QK_CONV, :] * raw.astype(F32)
    for j in range(QK_CONV - 1):
        acc = acc + w_ref[j:j + 1, :] * _dot(shift_ref[j], x_ref[...])
    x_ref[0:L, :] = raw
    return acc * jax.nn.sigmoid(acc)


def _row_to_col(row, eye):
    L = eye.shape[0]
    return jnp.sum(jnp.where(eye, jnp.broadcast_to(row, (L, L)), 0.0), axis=1, keepdims=True)


def _mlstm_init(C_ref, n_ref, m_ref, xq_ref, xk_ref):
    L = xq_ref.shape[0] // 2
    C_ref[...] = jnp.zeros_like(C_ref)
    n_ref[...] = jnp.zeros_like(n_ref)
    m_ref[...] = jnp.zeros_like(m_ref)
    xq_ref[0:L, :] = jnp.zeros((L, xq_ref.shape[1]), BF16)
    xk_ref[0:L, :] = jnp.zeros((L, xk_ref.shape[1]), BF16)


def _mlstm_body(bias_ref, q_ref, k_ref, v_ref, o_ref, g_ref, sh_ref, cwq_ref, cwk_ref, cbq_ref,
                cbk_ref, ng_ref, out_ref, C_ref, n_ref, m_ref, xq_ref, xk_ref, *, L, DK, DV):
    H = HEADS
    q_all = _conv_silu(q_ref, xq_ref, sh_ref, cwq_ref, cbq_ref, L) * (DK ** -0.5)
    k_all = _conv_silu(k_ref, xk_ref, sh_ref, cwk_ref, cbk_ref, L)

    G = g_ref[...] + bias_ref[...]
    bc = _log_sigmoid(G)
    lane = lax.broadcasted_iota(jnp.int32, (2 * H, L), 1)
    s = 1
    while s < L:
        bc = bc + jnp.where(lane >= s, pltpu.roll(bc, s, axis=1), 0.0)
        s *= 2

    row = lax.broadcasted_iota(jnp.int32, (L, L), 0)
    col = lax.broadcasted_iota(jnp.int32, (L, L), 1)
    eye = row == col
    causal = col <= row

    for h in range(H):
        q = q_all[:, h * DK:(h + 1) * DK]
        k = k_all[:, h * DK:(h + 1) * DK]
        v = v_ref[:, h * DV:(h + 1) * DV]
        gi = G[h:h + 1, :]
        b_row = bc[H + h:H + h + 1, :]
        b_col = _row_to_col(b_row, eye)

        m_prev = m_ref[h]
        C = C_ref[h]
        n = n_ref[h]
        D = jnp.where(causal, b_col - b_row + gi, NEG_INF)
        inter = b_col + m_prev
        m_t = jnp.maximum(inter, jnp.max(D, axis=1, keepdims=True))
        g = jnp.exp(inter - m_t)
        qb = q.astype(BF16)
        kb = k.astype(BF16)
        s_qk = _dot_nt(qb, kb) * jnp.exp(D - m_t)
        num = g * _dot(qb, C.astype(BF16)) + _dot(s_qk.astype(BF16), v)
        den = (g * jnp.sum(q * n, axis=1, keepdims=True)
               + jnp.sum(s_qk, axis=1, keepdims=True))
        hh = num / jnp.maximum(jnp.abs(den), jnp.exp(-m_t))

        y = (_rms(hh) * ng_ref[:, h * DV:(h + 1) * DV]
             * jax.nn.sigmoid(o_ref[:, h * DV:(h + 1) * DV].astype(F32)))
        out_ref[:, h * DV:(h + 1) * DV] = y.astype(out_ref.dtype)

        bL = b_row[:, L - 1:L]
        a_row = bL - b_row + gi
        m_new = jnp.maximum(bL + m_prev, jnp.max(a_row, axis=1, keepdims=True))
        ws_col = _row_to_col(jnp.exp(a_row - m_new), eye)
        gC = jnp.exp(bL + m_prev - m_new)
        kw = k * ws_col
        C_ref[h] = gC * C + _dot_tn(kw.astype(BF16), v)
        n_ref[h] = gC * n + jnp.sum(kw, axis=0, keepdims=True)
        m_ref[h] = m_new


def _gla_init(ST_ref, bs_ref):
    ST_ref[...] = jnp.zeros_like(ST_ref)
    bs_ref[...] = jnp.zeros_like(bs_ref)


def _gla_body(q_ref, k_ref, v_ref, gg_ref, sm_ref, w2_ref, ab_ref, ng_ref, out_ref,
              ST_ref, bs_ref, *, L, DK, DV):
    H = HEADS
    z = _dot_hi(sm_ref[...], w2_ref[...]) + ab_ref[...]
    la = _log_sigmoid(z) * G_TAU_INV
    row = lax.broadcasted_iota(jnp.int32, (L, L), 0)
    col = lax.broadcasted_iota(jnp.int32, (L, L), 1)
    tril = jnp.where(col <= row, 1.0, 0.0).astype(F32)
    b_all = _dot_hi(tril, la)
    bs_ref[HALO:HALO + L, :] = b_all
    pos_col = lax.broadcasted_iota(jnp.int32, (L, DK), 0)
    eye = row == col

    for h in range(H):
        dk = slice(h * DK, (h + 1) * DK)
        dv = slice(h * DV, (h + 1) * DV)
        b = b_all[:, dk]
        q = q_ref[:, dk].astype(F32) * (DK ** -0.5)
        k = k_ref[:, dk].astype(F32)
        v = v_ref[:, dv]
        ST = ST_ref[h]
        o = _dot_nt((q * jnp.exp(b)).astype(BF16), ST.astype(BF16))

        A = jnp.where(eye, _dot_nt(q.astype(BF16), k.astype(BF16)), 0.0)
        n = L
        while n >= 2:
            hn = n // 2
            pos = pos_col % n
            if n >= 8:
                R = jnp.concatenate(
                    [jnp.broadcast_to(b[i * n + hn - 1:i * n + hn, :], (n, DK))
                     for i in range(L // n)], axis=0)
            elif n == 4:
                R = jnp.where(pos == 0, bs_ref[HALO + 1:HALO + 1 + L, dk],
                              jnp.where(pos == 1, b,
                                        jnp.where(pos == 2, bs_ref[HALO - 1:HALO - 1 + L, dk],
                                                  bs_ref[HALO - 2:HALO - 2 + L, dk])))
            else:
                R = jnp.where(pos == 0, b, bs_ref[HALO - 1:HALO - 1 + L, dk])
            qn = q * jnp.exp(jnp.where(pos >= hn, b - R, NEG_INF))
            kn = k * jnp.exp(jnp.where(pos < hn, R - b, NEG_INF))
            S = _dot_nt(qn.astype(BF16), kn.astype(BF16))
            A = A + jnp.where((row // n) == (col // n), S, 0.0)
            n = hn
        o = o + _dot(A.astype(BF16), v)

        gg = gg_ref[:, dv].astype(F32)
        y = _rms(o) * ng_ref[:, dv] * (gg * jax.nn.sigmoid(gg))
        out_ref[:, dv] = y.astype(out_ref.dtype)

        bL = b[L - 1:L, :]
        kdec = k * jnp.exp(bL - b)
        ST_ref[h] = ST * jnp.exp(bL) + _dot_tn(v, kdec.astype(BF16))


N_MLSTM_IN, N_GLA_IN, N_MLSTM_SCRATCH = 12, 8, 5


def _scan_kernel(*refs, L, DK, DV, n_cast):
    n_in = N_MLSTM_IN + N_GLA_IN
    m_in = refs[:N_MLSTM_IN]
    g_in = refs[N_MLSTM_IN:n_in]
    src_ref = refs[n_in]
    ym_ref, yg_ref, dst_ref = refs[n_in + 1:n_in + 4]
    scratch = refs[n_in + 4:]
    m_scr, g_scr = scratch[:N_MLSTM_SCRATCH], scratch[N_MLSTM_SCRATCH:]

    @pl.when(pl.program_id(1) == 0)
    def _():
        _mlstm_init(*m_scr)
        _gla_init(*g_scr)

    @pl.when(pl.program_id(0) * pl.num_programs(1) + pl.program_id(1) < n_cast)
    def _():
        dst_ref[...] = src_ref[...].astype(dst_ref.dtype)

    _mlstm_body(*m_in, ym_ref, *m_scr, L=L, DK=DK, DV=DV)
    _gla_body(*g_in, yg_ref, *g_scr, L=L, DK=DK, DV=DV)


def _scans(proj_m, gates, bias, conv_w, conv_b, norm_m, proj_g, small, w2p, a_b, norm_g, w_cast,
           *, B, S, DK, DV, L):
    H = HEADS
    NC = S // L
    T = B * S
    rb = _cast_block(w_cast.shape[0], B * NC, LANES)
    n_cast = w_cast.shape[0] // rb
    cast_blk = pl.BlockSpec((rb, w_cast.shape[1]),
                            lambda b, c: (jnp.minimum(b * NC + c, n_cast - 1), 0))
    kern = functools.partial(_scan_kernel, L=L, DK=DK, DV=DV, n_cast=n_cast)
    rowblk = lambda b, c: b * NC + c
    t_idx = jnp.arange(L)[None, :, None]
    r_idx = jnp.arange(2 * L)[None, None, :]
    lag = (QK_CONV - 1 - jnp.arange(QK_CONV - 1))[:, None, None]
    shifts = (r_idx == L + t_idx - lag).astype(BF16)
    qk = lambda j: pl.BlockSpec((L, H * DK), lambda b, c: (rowblk(b, c), j))
    vo = lambda j: pl.BlockSpec((L, H * DV), lambda b, c: (rowblk(b, c), j))
    fixed = lambda shape: pl.BlockSpec(shape, lambda b, c: (0,) * len(shape))
    out_spec = pl.BlockSpec((L, H * DV), lambda b, c: (rowblk(b, c), 0))
    out_sds = jax.ShapeDtypeStruct((T, H * DV), BF16)
    return pl.pallas_call(
        kern,
        grid=(B, NC),
        in_specs=[
            fixed((2 * H, 1)),
            qk(0), qk(1), vo(1), vo(2),
            pl.BlockSpec((None, None, 2 * H, L), lambda b, c: (b, c, 0, 0)),
            fixed((QK_CONV - 1, L, 2 * L)),
            pl.BlockSpec((QK_CONV, H * DK), lambda b, c: (0, 0)),
            pl.BlockSpec((QK_CONV, H * DK), lambda b, c: (0, 1)),
            pl.BlockSpec((1, H * DK), lambda b, c: (0, 0)),
            pl.BlockSpec((1, H * DK), lambda b, c: (0, 1)),
            fixed((1, H * DV)),
            qk(0), qk(1), vo(1), vo(2),
            pl.BlockSpec((L, SMALL_W), lambda b, c: (rowblk(b, c), 0)),
            fixed((SMALL_W, H * DK)),
            fixed((1, H * DK)),
            fixed((1, H * DV)),
            cast_blk,
        ],
        out_specs=[out_spec, out_spec, cast_blk],
        out_shape=[out_sds, out_sds, jax.ShapeDtypeStruct(w_cast.shape, BF16)],
        scratch_shapes=[
            pltpu.VMEM((H, DK, DV), F32),
            pltpu.VMEM((H, 1, DK), F32),
            pltpu.VMEM((H, 1, 1), F32),
            pltpu.VMEM((2 * L, H * DK), BF16),
            pltpu.VMEM((2 * L, H * DK), BF16),
            pltpu.VMEM((H, DV, DK), F32),
            pltpu.VMEM((L + 2 * HALO, H * DK), F32),
        ],
        compiler_params=_cparams(("arbitrary", "arbitrary")),
        name="scans",
    )(bias, proj_m, proj_m, proj_m, proj_m, gates, shifts, conv_w, conv_w, conv_b, conv_b, norm_m,
      proj_g, proj_g, proj_g, proj_g, small, w2p, a_b, norm_g, w_cast)


def _ffn_up_kernel(h_ref, wg_ref, wu_ref, cw_ref, cb_ref, wd_ref, act_ref, wdb_ref, gbuf_ref, ghalo_ref,
                   *, tm, S, tg, groups_last, n_cast):
    i = pl.program_id(0)
    j = pl.program_id(1)
    nj = pl.num_programs(1)
    ng = act_ref.shape[1] // tg
    seq_start = (i * tm) % S == 0

    @pl.when(jnp.logical_and(i == 0, j == 0))
    def _():
        ghalo_ref[...] = jnp.zeros_like(ghalo_ref)

    @pl.when(i * nj + j < n_cast)
    def _():
        wdb_ref[...] = wd_ref[...].astype(wdb_ref.dtype)

    def body(n_groups):
        hb = h_ref[...]
        silu = []
        for g in range(n_groups):
            cols = slice(g * tg, (g + 1) * tg)
            gate = _dot(hb, wg_ref[:, cols])
            gbuf_ref[g, 0:HALO, :] = jnp.where(seq_start, 0.0, ghalo_ref[j, :, cols])
            gbuf_ref[g, HALO:HALO + tm, :] = gate
            ghalo_ref[j, :, cols] = gate[tm - HALO:tm, :]
            conv = cb_ref[:, cols] + cw_ref[FFN_CONV - 1:FFN_CONV, cols] * gate
            for t in range(FFN_CONV - 1):
                off = HALO - (FFN_CONV - 1) + t
                conv = conv + cw_ref[t:t + 1, cols] * gbuf_ref[g, off:off + tm, :]
            silu.append(conv * jax.nn.sigmoid(conv))
        for g in range(n_groups):
            cols = slice(g * tg, (g + 1) * tg)
            up = _dot(hb, wu_ref[:, cols])
            act_ref[:, cols] = (silu[g] * up).astype(act_ref.dtype)

    if groups_last == ng:
        body(ng)
    else:
        pl.when(j < nj - 1)(functools.partial(body, ng))
        pl.when(j == nj - 1)(functools.partial(body, groups_last))


def _ffn_up(h2, wg, wu, conv_w, conv_b, wd, *, S):
    T, D = h2.shape
    F = wg.shape[1]
    tm = _pick(S, (1024, 512, 256, 128))
    tg = _pick(F, (MXU_COLS_V7X, 128))
    tn = FFN_TN
    nj = -(-F // tn)
    groups_last = (F - (nj - 1) * tn) // tg
    rb = _cast_block(F, (T // tm) * nj, LANES)
    n_cast = F // rb
    kern = functools.partial(_ffn_up_kernel, tm=tm, S=S, tg=tg, groups_last=groups_last,
                             n_cast=n_cast)
    wd_blk = pl.BlockSpec((rb, D), lambda i, j: (jnp.minimum(i * nj + j, n_cast - 1), 0))
    return pl.pallas_call(
        kern,
        grid=(T // tm, nj),
        in_specs=[
            pl.BlockSpec((tm, D), lambda i, j: (i, 0)),
            pl.BlockSpec((D, tn), lambda i, j: (0, j)),
            pl.BlockSpec((D, tn), lambda i, j: (0, j)),
            pl.BlockSpec((FFN_CONV, tn), lambda i, j: (0, j)),
            pl.BlockSpec((1, tn), lambda i, j: (0, j)),
            wd_blk,
        ],
        out_specs=[pl.BlockSpec((tm, tn), lambda i, j: (i, j)), wd_blk],
        out_shape=[jax.ShapeDtypeStruct((T, F), BF16), jax.ShapeDtypeStruct((F, D), BF16)],
        scratch_shapes=[pltpu.VMEM((tn // tg, tm + HALO, tg), F32),
                        pltpu.VMEM((nj, HALO, tn), F32)],
        compiler_params=_cparams(("arbitrary", "arbitrary")),
        name="ffn_up",
    )(h2, wg, wu, conv_w, conv_b, wd)


def _ffn_down_kernel(a_ref, wd_ref, x_ref, gf_ref, out_ref, *, tn, k_tail):
    k = pl.program_id(1)
    nk = pl.num_programs(1)
    tk = a_ref.shape[1]
    D = out_ref.shape[1]

    def accumulate(kk, first):
        a = a_ref[:, :kk]
        for n0 in range(0, D, tn):
            d = _dot(a, wd_ref[:kk, n0:n0 + tn])
            if first:
                out_ref[:, n0:n0 + tn] = x_ref[:, n0:n0 + tn] + d
            else:
                out_ref[:, n0:n0 + tn] += d

    pl.when(k == 0)(functools.partial(accumulate, tk, True))
    pl.when(jnp.logical_and(k > 0, k < nk - 1))(functools.partial(accumulate, tk, False))
    pl.when(k == nk - 1)(functools.partial(accumulate, k_tail, False))

    @pl.when(k == nk - 1)
    def _():
        out_ref[...] = _rms(out_ref[...]) * gf_ref[...]


def _ffn_down(act, wd, x1, lnf_g):
    T, F = act.shape
    D = wd.shape[1]
    tm = _pick(T, (512, 256, 128))
    tn = _pick(D, (1024, 512, 256, 128))
    tk = FFN_TK
    nk = -(-F // tk)
    assert nk >= 2
    kern = functools.partial(_ffn_down_kernel, tn=tn, k_tail=F - (nk - 1) * tk)
    return pl.pallas_call(
        kern,
        grid=(T // tm, nk),
        in_specs=[
            pl.BlockSpec((tm, tk), lambda i, k: (i, k)),
            pl.BlockSpec((tk, D), lambda i, k: (k, 0)),
            pl.BlockSpec((tm, D), lambda i, k: (i, 0)),
            pl.BlockSpec((1, D), lambda i, k: (0, 0)),
        ],
        out_specs=pl.BlockSpec((tm, D), lambda i, k: (i, 0)),
        out_shape=jax.ShapeDtypeStruct((T, D), F32),
        compiler_params=_cparams(("parallel", "arbitrary")),
        name="ffn_down",
    )(act, wd, x1, lnf_g)


def kernel(x, ln1_g, w_in, mlstm_conv_w, mlstm_conv_b, mlstm_i_b, mlstm_f_b, mlstm_norm_g,
           gla_a_w2, gla_a_b, gla_norm_g, w_out, ln2_g, w_ffn_gate, w_ffn_up,
           ffn_conv_w, ffn_conv_b, w_ffn_down, lnf_g):
    B, S, D = x.shape
    T = B * S
    H = HEADS
    M_DK = mlstm_conv_w.shape[1] // (2 * H)
    M_DV = mlstm_norm_g.shape[0] // H
    G_DK = gla_a_w2.shape[1] // H
    G_DV = gla_norm_g.shape[0] // H
    F = w_ffn_gate.shape[1]
    assert M_DV == 2 * M_DK and G_DV == 2 * G_DK and M_DK == G_DK

    n_m = 2 * H * M_DK + 2 * H * M_DV
    n_g = 2 * H * G_DK + 2 * H * G_DV
    o_gate = n_m
    o_g = n_m + 2 * H
    o_a = o_g + n_g
    assert w_in.shape[1] == o_a + G_RANK

    w_in_t = w_in.T.astype(BF16)
    w_g = w_in_t[o_g:o_a]
    w_small = jnp.concatenate(
        [w_in_t[o_gate:o_g], w_in_t[o_a:],
         jnp.zeros((SMALL_W - 2 * H - G_RANK, D), BF16)], axis=0)

    x2d = x.reshape(T, D)
    h1 = _rmsnorm(x2d, ln1_g, BF16)
    proj_m, wg_b = _matmul_nt(h1, w_in_t, BF16, "inproj_mlstm", n_out=n_m,
                              cast_src=w_ffn_gate)
    proj_g, wu_b = _matmul_nt(h1, w_g, BF16, "inproj_gla", cast_src=w_ffn_up)
    small = _matmul_nt(h1, w_small, F32, "inproj_small")

    L = 128
    gates = small[:, :2 * H].reshape(B, S // L, L, 2 * H)
    gates = jnp.transpose(gates, (0, 1, 3, 2))
    bias = jnp.concatenate([mlstm_i_b, mlstm_f_b]).astype(F32).reshape(2 * H, 1)
    w2p = jnp.zeros((SMALL_W, H * G_DK), F32).at[2 * H:2 * H + G_RANK].set(gla_a_w2.astype(F32))
    ym, yg, w_out_b = _scans(proj_m, gates, bias, mlstm_conv_w.astype(F32),
                             mlstm_conv_b.reshape(1, -1).astype(F32),
                             mlstm_norm_g.reshape(1, -1).astype(F32),
                             proj_g, small, w2p, gla_a_b.reshape(1, -1).astype(F32),
                             gla_norm_g.reshape(1, -1).astype(F32), w_out,
                             B=B, S=S, DK=M_DK, DV=M_DV, L=L)

    x1 = _outproj(ym, yg, w_out_b, x2d)

    h2 = _rmsnorm(x1, ln2_g, BF16)
    act, wd_b = _ffn_up(h2, wg_b, wu_b, ffn_conv_w.astype(F32),
                        ffn_conv_b.reshape(1, -1).astype(F32), w_ffn_down, S=S)
    out = _ffn_down(act, wd_b, x1, lnf_g.reshape(1, -1).astype(F32))
    return out.reshape(B, S, D)
```

```python
import functools

import jax
import jax.numpy as jnp
from jax import lax
from jax.experimental import pallas as pl
from jax.experimental.pallas import tpu as pltpu

F32 = jnp.float32
BF16 = jnp.bfloat16

HEADS = 4
QK_CONV = 4
FFN_CONV = 3
G_RANK = 16
G_TAU_INV = 1.0 / 16.0
EPS = 1e-6
SMALL_W = 128
HALO = 8
NEG_INF = float("-inf")
FFN_TN = 512
FFN_TK = 1024
MXU_COLS_V7X = 256
LANES = 128

VMEM_LIMIT_V7X = 56 * 1024 * 1024


def _cparams(sem, vmem=VMEM_LIMIT_V7X):
    return pltpu.CompilerParams(dimension_semantics=sem, vmem_limit_bytes=vmem)


def _pick(n, prefs):
    for p in prefs:
        if n % p == 0:
            return p
    raise ValueError(f"no tile in {prefs} divides {n}")


def _log_sigmoid(x):
    return jnp.minimum(x, 0.0) - jnp.log1p(jnp.exp(-jnp.abs(x)))


def _dot(a, b):
    return jnp.dot(a, b, preferred_element_type=F32)


def _dot_hi(a, b):
    return jnp.dot(a, b, preferred_element_type=F32, precision=lax.Precision.HIGHEST)


def _dot_nt(a, b):
    return lax.dot_general(a, b, (((1,), (1,)), ((), ())), preferred_element_type=F32)


def _dot_tn(a, b):
    return lax.dot_general(a, b, (((0,), (0,)), ((), ())), preferred_element_type=F32)


def _rms(x):
    return x * lax.rsqrt(jnp.mean(x * x, axis=-1, keepdims=True) + EPS)


def _rmsnorm_kernel(x_ref, g_ref, o_ref):
    o_ref[...] = (_rms(x_ref[...].astype(F32)) * g_ref[...]).astype(o_ref.dtype)


def _rmsnorm(x, g, out_dtype):
    T, D = x.shape
    tr = _pick(T, (512, 256, 128, 64, 8))
    return pl.pallas_call(
        _rmsnorm_kernel,
        grid=(T // tr,),
        in_specs=[pl.BlockSpec((tr, D), lambda i: (i, 0)),
                  pl.BlockSpec((1, D), lambda i: (0, 0))],
        out_specs=pl.BlockSpec((tr, D), lambda i: (i, 0)),
        out_shape=jax.ShapeDtypeStruct((T, D), out_dtype),
        compiler_params=_cparams(("parallel",)),
        name="rmsnorm",
    )(x, g.reshape(1, D).astype(F32))


def _cast_block(n, steps, min_block):
    cands = [c for c in range(LANES, n + 1, LANES) if n % c == 0 and n // c <= steps]
    wide = [c for c in cands if c >= min_block]
    return (wide or cands)[0]


def _matmul_nt_kernel(a_ref, wt_ref, *rest, n_cast):
    o_ref = rest[-2] if n_cast else rest[-1]
    o_ref[...] = _dot_nt(a_ref[...], wt_ref[...]).astype(o_ref.dtype)
    if n_cast:
        src_ref, dst_ref = rest[0], rest[-1]
        step = pl.program_id(0) * pl.num_programs(1) + pl.program_id(1)

        @pl.when(step < n_cast)
        def _():
            dst_ref[...] = src_ref[...].astype(dst_ref.dtype)


def _matmul_nt(a, wt, out_dtype, name, n_out=None, cast_src=None):
    M, K = a.shape
    N = wt.shape[0] if n_out is None else n_out
    tm = _pick(M, (1024, 512, 256, 128))
    tn = _pick(N, (1024, 512, 256, 128))
    grid = (M // tm, N // tn)
    in_specs = [pl.BlockSpec((tm, K), lambda i, j: (i, 0)),
                pl.BlockSpec((tn, K), lambda i, j: (j, 0))]
    out_specs = [pl.BlockSpec((tm, tn), lambda i, j: (i, j))]
    out_shape = [jax.ShapeDtypeStruct((M, N), out_dtype)]
    args = [a, wt]
    n_cast = 0
    if cast_src is not None:
        R, C = cast_src.shape
        cb = _cast_block(C, grid[0] * grid[1], 2 * LANES)
        n_cast = C // cb
        blk = pl.BlockSpec((R, cb), lambda i, j: (0, jnp.minimum(i * grid[1] + j, n_cast - 1)))
        in_specs.append(blk)
        out_specs.append(blk)
        out_shape.append(jax.ShapeDtypeStruct((R, C), BF16))
        args.append(cast_src)
    res = pl.pallas_call(
        functools.partial(_matmul_nt_kernel, n_cast=n_cast),
        grid=grid,
        in_specs=in_specs,
        out_specs=out_specs,
        out_shape=out_shape,
        compiler_params=_cparams(("arbitrary", "arbitrary")),
        name=name,
    )(*args)
    return res if n_cast else res[0]


def _outproj_kernel(ym_ref, yg_ref, wt_ref, wb_ref, x_ref, o_ref):
    acc = _dot(ym_ref[...], wt_ref[...]) + _dot(yg_ref[...], wb_ref[...])
    o_ref[...] = x_ref[...] + acc


def _outproj(ym, yg, w, x):
    M, K2 = ym.shape
    N = w.shape[1]
    tm = _pick(M, (1024, 512, 256, 128))
    tn = _pick(N, (1024, 512, 256, 128))
    return pl.pallas_call(
        _outproj_kernel,
        grid=(M // tm, N // tn),
        in_specs=[pl.BlockSpec((tm, K2), lambda i, j: (i, 0)),
                  pl.BlockSpec((tm, K2), lambda i, j: (i, 0)),
                  pl.BlockSpec((K2, tn), lambda i, j: (0, j)),
                  pl.BlockSpec((K2, tn), lambda i, j: (1, j)),
                  pl.BlockSpec((tm, tn), lambda i, j: (i, j))],
        out_specs=pl.BlockSpec((tm, tn), lambda i, j: (i, j)),
        out_shape=jax.ShapeDtypeStruct((M, N), F32),
        compiler_params=_cparams(("parallel", "arbitrary")),
        name="outproj",
    )(ym, yg, w, w, x)


def _conv_silu(raw_ref, x_ref, shift_ref, w_ref, b_ref, L):
    raw = raw_ref[...]
    x_ref[L:2 * L, :] = raw
    acc = b_ref[...] + w_ref[QK_CONV - 1:QK_CONV, :] * raw.astype(F32)
    for j in range(QK_CONV - 1):
        acc = acc + w_ref[j:j + 1, :] * _dot(shift_ref[j], x_ref[...])
    x_ref[0:L, :] = raw
    return acc * jax.nn.sigmoid(acc)


def _row_to_col(row, eye):
    L = eye.shape[0]
    return jnp.sum(jnp.where(eye, jnp.broadcast_to(row, (L, L)), 0.0), axis=1, keepdims=True)


def _mlstm_init(C_ref, n_ref, m_ref, xq_ref, xk_ref):
    L = xq_ref.shape[0] // 2
    C_ref[...] = jnp.zeros_like(C_ref)
    n_ref[...] = jnp.zeros_like(n_ref)
    m_ref[...] = jnp.zeros_like(m_ref)
    xq_ref[0:L, :] = jnp.zeros((L, xq_ref.shape[1]), BF16)
    xk_ref[0:L, :] = jnp.zeros((L, xk_ref.shape[1]), BF16)


def _mlstm_body(bias_ref, q_ref, k_ref, v_ref, o_ref, g_ref, sh_ref, cwq_ref, cwk_ref, cbq_ref,
                cbk_ref, ng_ref, out_ref, C_ref, n_ref, m_ref, xq_ref, xk_ref, *, L, DK, DV):
    H = HEADS
    q_all = _conv_silu(q_ref, xq_ref, sh_ref, cwq_ref, cbq_ref, L) * (DK ** -0.5)
    k_all = _conv_silu(k_ref, xk_ref, sh_ref, cwk_ref, cbk_ref, L)

    G = g_ref[...] + bias_ref[...]
    bc = _log_sigmoid(G)
    lane = lax.broadcasted_iota(jnp.int32, (2 * H, L), 1)
    s = 1
    while s < L:
        bc = bc + jnp.where(lane >= s, pltpu.roll(bc, s, axis=1), 0.0)
        s *= 2

    row = lax.broadcasted_iota(jnp.int32, (L, L), 0)
    col = lax.broadcasted_iota(jnp.int32, (L, L), 1)
    eye = row == col
    causal = col <= row

    for h in range(H):
        q = q_all[:, h * DK:(h + 1) * DK]
        k = k_all[:, h * DK:(h + 1) * DK]
        v = v_ref[:, h * DV:(h + 1) * DV]
        gi = G[h:h + 1, :]
        b_row = bc[H + h:H + h + 1, :]
        b_col = _row_to_col(b_row, eye)

        m_prev = m_ref[h]
        C = C_ref[h]
        n = n_ref[h]
        D = jnp.where(causal, b_col - b_row + gi, NEG_INF)
        inter = b_col + m_prev
        m_t = jnp.maximum(inter, jnp.max(D, axis=1, keepdims=True))
        g = jnp.exp(inter - m_t)
        qb = q.astype(BF16)
        kb = k.astype(BF16)
        s_qk = _dot_nt(qb, kb) * jnp.exp(D - m_t)
        num = g * _dot(qb, C.astype(BF16)) + _dot(s_qk.astype(BF16), v)
        den = (g * jnp.sum(q * n, axis=1, keepdims=True)
               + jnp.sum(s_qk, axis=1, keepdims=True))
        hh = num / jnp.maximum(jnp.abs(den), jnp.exp(-m_t))

        y = (_rms(hh) * ng_ref[:, h * DV:(h + 1) * DV]
             * jax.nn.sigmoid(o_ref[:, h * DV:(h + 1) * DV].astype(F32)))
        out_ref[:, h * DV:(h + 1) * DV] = y.astype(out_ref.dtype)

        bL = b_row[:, L - 1:L]
        a_row = bL - b_row + gi
        m_new = jnp.maximum(bL + m_prev, jnp.max(a_row, axis=1, keepdims=True))
        ws_col = _row_to_col(jnp.exp(a_row - m_new), eye)
        gC = jnp.exp(bL + m_prev - m_new)
        kw = k * ws_col
        C_ref[h] = gC * C + _dot_tn(kw.astype(BF16), v)
        n_ref[h] = gC * n + jnp.sum(kw, axis=0, keepdims=True)
        m_ref[h] = m_new


def _gla_init(ST_ref, bs_ref):
    ST_ref[...] = jnp.zeros_like(ST_ref)
    bs_ref[...] = jnp.zeros_like(bs_ref)


def _gla_body(q_ref, k_ref, v_ref, gg_ref, sm_ref, w2_ref, ab_ref, ng_ref, out_ref,
              ST_ref, bs_ref, *, L, DK, DV):
    H = HEADS
    z = _dot_hi(sm_ref[...], w2_ref[...]) + ab_ref[...]
    la = _log_sigmoid(z) * G_TAU_INV
    row = lax.broadcasted_iota(jnp.int32, (L, L), 0)
    col = lax.broadcasted_iota(jnp.int32, (L, L), 1)
    tril = jnp.where(col <= row, 1.0, 0.0).astype(F32)
    b_all = _dot_hi(tril, la)
    bs_ref[HALO:HALO + L, :] = b_all
    pos_col = lax.broadcasted_iota(jnp.int32, (L, DK), 0)
    eye = row == col

    for h in range(H):
        dk = slice(h * DK, (h + 1) * DK)
        dv = slice(h * DV, (h + 1) * DV)
        b = b_all[:, dk]
        q = q_ref[:, dk].astype(F32) * (DK ** -0.5)
        k = k_ref[:, dk].astype(F32)
        v = v_ref[:, dv]
        ST = ST_ref[h]
        o = _dot((q * jnp.exp(b)).astype(BF16), ST.astype(BF16))

        A = jnp.where(eye, _dot_nt(q.astype(BF16), k.astype(BF16)), 0.0)
        n = L
        while n >= 2:
            hn = n // 2
            pos = pos_col % n
            if n >= 8:
                R = jnp.concatenate(
                    [jnp.broadcast_to(b[i * n + hn - 1:i * n + hn, :], (n, DK))
                     for i in range(L // n)], axis=0)
            elif n == 4:
                R = jnp.where(pos == 0, bs_ref[HALO + 1:HALO + 1 + L, dk],
                              jnp.where(pos == 1, b,
                                        jnp.where(pos == 2, bs_ref[HALO - 1:HALO - 1 + L, dk],
                                                  bs_ref[HALO - 2:HALO - 2 + L, dk])))
            else:
                R = jnp.where(pos == 0, b, bs_ref[HALO - 1:HALO - 1 + L, dk])
            qn = q * jnp.exp(jnp.where(pos >= hn, b - R, NEG_INF))
            kn = k * jnp.exp(jnp.where(pos < hn, R - b, NEG_INF))
            S = _dot_nt(qn.astype(BF16), kn.astype(BF16))
            A = A + jnp.where((row // n) == (col // n), S, 0.0)
            n = hn
        o = o + _dot(A.astype(BF16), v)

        gg = gg_ref[:, dv].astype(F32)
        y = _rms(o) * ng_ref[:, dv] * (gg * jax.nn.sigmoid(gg))
        out_ref[:, dv] = y.astype(out_ref.dtype)

        bL = b[L - 1:L, :]
        kdec = k * jnp.exp(bL - b)
        decay_col = jnp.transpose(jnp.broadcast_to(jnp.exp(bL), (LANES, DK)))[:, 0:1]
        ST_ref[h] = ST * decay_col + _dot_tn(kdec.astype(BF16), v)


N_MLSTM_IN, N_GLA_IN, N_MLSTM_SCRATCH = 12, 8, 5


def _scan_kernel(*refs, L, DK, DV, n_cast):
    n_in = N_MLSTM_IN + N_GLA_IN
    m_in = refs[:N_MLSTM_IN]
    g_in = refs[N_MLSTM_IN:n_in]
    src_ref = refs[n_in]
    ym_ref, yg_ref, dst_ref = refs[n_in + 1:n_in + 4]
    scratch = refs[n_in + 4:]
    m_scr, g_scr = scratch[:N_MLSTM_SCRATCH], scratch[N_MLSTM_SCRATCH:]

    @pl.when(pl.program_id(1) == 0)
    def _():
        _mlstm_init(*m_scr)
        _gla_init(*g_scr)

    @pl.when(pl.program_id(0) * pl.num_programs(1) + pl.program_id(1) < n_cast)
    def _():
        dst_ref[...] = src_ref[...].astype(dst_ref.dtype)

    _mlstm_body(*m_in, ym_ref, *m_scr, L=L, DK=DK, DV=DV)
    _gla_body(*g_in, yg_ref, *g_scr, L=L, DK=DK, DV=DV)


def _scans(proj_m, gates, bias, conv_w, conv_b, norm_m, proj_g, small, w2p, a_b, norm_g, w_cast,
           *, B, S, DK, DV, L):
    H = HEADS
    NC = S // L
    T = B * S
    rb = _cast_block(w_cast.shape[0], B * NC, LANES)
    n_cast = w_cast.shape[0] // rb
    cast_blk = pl.BlockSpec((rb, w_cast.shape[1]),
                            lambda b, c: (jnp.minimum(b * NC + c, n_cast - 1), 0))
    kern = functools.partial(_scan_kernel, L=L, DK=DK, DV=DV, n_cast=n_cast)
    rowblk = lambda b, c: b * NC + c
    t_idx = jnp.arange(L)[None, :, None]
    r_idx = jnp.arange(2 * L)[None, None, :]
    lag = (QK_CONV - 1 - jnp.arange(QK_CONV - 1))[:, None, None]
    shifts = (r_idx == L + t_idx - lag).astype(BF16)
    qk = lambda j: pl.BlockSpec((L, H * DK), lambda b, c: (rowblk(b, c), j))
    vo = lambda j: pl.BlockSpec((L, H * DV), lambda b, c: (rowblk(b, c), j))
    fixed = lambda shape: pl.BlockSpec(shape, lambda b, c: (0,) * len(shape))
    out_spec = pl.BlockSpec((L, H * DV), lambda b, c: (rowblk(b, c), 0))
    out_sds = jax.ShapeDtypeStruct((T, H * DV), BF16)
    return pl.pallas_call(
        kern,
        grid=(B, NC),
        in_specs=[
            fixed((2 * H, 1)),
            qk(0), qk(1), vo(1), vo(2),
            pl.BlockSpec((None, None, 2 * H, L), lambda b, c: (b, c, 0, 0)),
            fixed((QK_CONV - 1, L, 2 * L)),
            pl.BlockSpec((QK_CONV, H * DK), lambda b, c: (0, 0)),
            pl.BlockSpec((QK_CONV, H * DK), lambda b, c: (0, 1)),
            pl.BlockSpec((1, H * DK), lambda b, c: (0, 0)),
            pl.BlockSpec((1, H * DK), lambda b, c: (0, 1)),
            fixed((1, H * DV)),
            qk(0), qk(1), vo(1), vo(2),
            pl.BlockSpec((L, SMALL_W), lambda b, c: (rowblk(b, c), 0)),
            fixed((SMALL_W, H * DK)),
            fixed((1, H * DK)),
            fixed((1, H * DV)),
            cast_blk,
        ],
        out_specs=[out_spec, out_spec, cast_blk],
        out_shape=[out_sds, out_sds, jax.ShapeDtypeStruct(w_cast.shape, BF16)],
        scratch_shapes=[
            pltpu.VMEM((H, DK, DV), F32),
            pltpu.VMEM((H, 1, DK), F32),
            pltpu.VMEM((H, 1, 1), F32),
            pltpu.VMEM((2 * L, H * DK), BF16),
            pltpu.VMEM((2 * L, H * DK), BF16),
            pltpu.VMEM((H, DK, DV), F32),
            pltpu.VMEM((L + 2 * HALO, H * DK), F32),
        ],
        compiler_params=_cparams(("arbitrary", "arbitrary")),
        name="scans",
    )(bias, proj_m, proj_m, proj_m, proj_m, gates, shifts, conv_w, conv_w, conv_b, conv_b, norm_m,
      proj_g, proj_g, proj_g, proj_g, small, w2p, a_b, norm_g, w_cast)


def _ffn_up_kernel(h_ref, wg_ref, wu_ref, cw_ref, cb_ref, wd_ref, act_ref, wdb_ref, gbuf_ref, ghalo_ref,
                   *, tm, S, tg, groups_last, n_cast):
    i = pl.program_id(0)
    j = pl.program_id(1)
    nj = pl.num_programs(1)
    ng = act_ref.shape[1] // tg
    seq_start = (i * tm) % S == 0

    @pl.when(jnp.logical_and(i == 0, j == 0))
    def _():
        ghalo_ref[...] = jnp.zeros_like(ghalo_ref)

    @pl.when(i * nj + j < n_cast)
    def _():
        wdb_ref[...] = wd_ref[...].astype(wdb_ref.dtype)

    def body(n_groups):
        hb = h_ref[...]
        silu = []
        for g in range(n_groups):
            cols = slice(g * tg, (g + 1) * tg)
            gate = _dot(hb, wg_ref[:, cols])
            gbuf_ref[g, 0:HALO, :] = jnp.where(seq_start, 0.0, ghalo_ref[j, :, cols])
            gbuf_ref[g, HALO:HALO + tm, :] = gate
            ghalo_ref[j, :, cols] = gate[tm - HALO:tm, :]
            conv = cb_ref[:, cols] + cw_ref[FFN_CONV - 1:FFN_CONV, cols] * gate
            for t in range(FFN_CONV - 1):
                off = HALO - (FFN_CONV - 1) + t
                conv = conv + cw_ref[t:t + 1, cols] * gbuf_ref[g, off:off + tm, :]
            silu.append(conv * jax.nn.sigmoid(conv))
        for g in range(n_groups):
            cols = slice(g * tg, (g + 1) * tg)
            up = _dot(hb, wu_ref[:, cols])
            act_ref[:, cols] = (silu[g] * up).astype(act_ref.dtype)

    if groups_last == ng:
        body(ng)
    else:
        pl.when(j < nj - 1)(functools.partial(body, ng))
        pl.when(j == nj - 1)(functools.partial(body, groups_last))


def _ffn_up(h2, wg, wu, conv_w, conv_b, wd, *, S):
    T, D = h2.shape
    F = wg.shape[1]
    tm = _pick(S, (1024, 512, 256, 128))
    tg = _pick(F, (MXU_COLS_V7X, 128))
    tn = FFN_TN
    nj = -(-F // tn)
    groups_last = (F - (nj - 1) * tn) // tg
    rb = _cast_block(F, (T // tm) * nj, LANES)
    n_cast = F // rb
    kern = functools.partial(_ffn_up_kernel, tm=tm, S=S, tg=tg, groups_last=groups_last,
                             n_cast=n_cast)
    wd_blk = pl.BlockSpec((rb, D), lambda i, j: (jnp.minimum(i * nj + j, n_cast - 1), 0))
    return pl.pallas_call(
        kern,
        grid=(T // tm, nj),
        in_specs=[
            pl.BlockSpec((tm, D), lambda i, j: (i, 0)),
            pl.BlockSpec((D, tn), lambda i, j: (0, j)),
            pl.BlockSpec((D, tn), lambda i, j: (0, j)),
            pl.BlockSpec((FFN_CONV, tn), lambda i, j: (0, j)),
            pl.BlockSpec((1, tn), lambda i, j: (0, j)),
            wd_blk,
        ],
        out_specs=[pl.BlockSpec((tm, tn), lambda i, j: (i, j)), wd_blk],
        out_shape=[jax.ShapeDtypeStruct((T, F), BF16), jax.ShapeDtypeStruct((F, D), BF16)],
        scratch_shapes=[pltpu.VMEM((tn // tg, tm + HALO, tg), F32),
                        pltpu.VMEM((nj, HALO, tn), F32)],
        compiler_params=_cparams(("arbitrary", "arbitrary")),
        name="ffn_up",
    )(h2, wg, wu, conv_w, conv_b, wd)


def _ffn_down_kernel(a_ref, wd_ref, x_ref, gf_ref, out_ref, *, tn, k_tail):
    k = pl.program_id(1)
    nk = pl.num_programs(1)
    tk = a_ref.shape[1]
    D = out_ref.shape[1]

    def accumulate(kk, first):
        a = a_ref[:, :kk]
        for n0 in range(0, D, tn):
            d = _dot(a, wd_ref[:kk, n0:n0 + tn])
            if first:
                out_ref[:, n0:n0 + tn] = x_ref[:, n0:n0 + tn] + d
            else:
                out_ref[:, n0:n0 + tn] += d

    pl.when(k == 0)(functools.partial(accumulate, tk, True))
    pl.when(jnp.logical_and(k > 0, k < nk - 1))(functools.partial(accumulate, tk, False))
    pl.when(k == nk - 1)(functools.partial(accumulate, k_tail, False))

    @pl.when(k == nk - 1)
    def _():
        out_ref[...] = _rms(out_ref[...]) * gf_ref[...]


def _ffn_down(act, wd, x1, lnf_g):
    T, F = act.shape
    D = wd.shape[1]
    tm = _pick(T, (512, 256, 128))
    tn = _pick(D, (1024, 512, 256, 128))
    tk = FFN_TK
    nk = -(-F // tk)
    assert nk >= 2
    kern = functools.partial(_ffn_down_kernel, tn=tn, k_tail=F - (nk - 1) * tk)
    return pl.pallas_call(
        kern,
        grid=(T // tm, nk),
        in_specs=[
            pl.BlockSpec((tm, tk), lambda i, k: (i, k)),
            pl.BlockSpec((tk, D), lambda i, k: (k, 0)),
            pl.BlockSpec((tm, D), lambda i, k: (i, 0)),
            pl.BlockSpec((1, D), lambda i, k: (0, 0)),
        ],
        out_specs=pl.BlockSpec((tm, D), lambda i, k: (i, 0)),
        out_shape=jax.ShapeDtypeStruct((T, D), F32),
        compiler_params=_cparams(("parallel", "arbitrary")),
        name="ffn_down",
    )(act, wd, x1, lnf_g)


def kernel(x, ln1_g, w_in, mlstm_conv_w, mlstm_conv_b, mlstm_i_b, mlstm_f_b, mlstm_norm_g,
           gla_a_w2, gla_a_b, gla_norm_g, w_out, ln2_g, w_ffn_gate, w_ffn_up,
           ffn_conv_w, ffn_conv_b, w_ffn_down, lnf_g):
    B, S, D = x.shape
    T = B * S
    H = HEADS
    M_DK = mlstm_conv_w.shape[1] // (2 * H)
    M_DV = mlstm_norm_g.shape[0] // H
    G_DK = gla_a_w2.shape[1] // H
    G_DV = gla_norm_g.shape[0] // H
    F = w_ffn_gate.shape[1]
    assert M_DV == 2 * M_DK and G_DV == 2 * G_DK and M_DK == G_DK

    n_m = 2 * H * M_DK + 2 * H * M_DV
    n_g = 2 * H * G_DK + 2 * H * G_DV
    o_gate = n_m
    o_g = n_m + 2 * H
    o_a = o_g + n_g
    assert w_in.shape[1] == o_a + G_RANK

    w_in_t = w_in.T.astype(BF16)
    w_g = w_in_t[o_g:o_a]
    w_small = jnp.concatenate(
        [w_in_t[o_gate:o_g], w_in_t[o_a:],
         jnp.zeros((SMALL_W - 2 * H - G_RANK, D), BF16)], axis=0)

    x2d = x.reshape(T, D)
    h1 = _rmsnorm(x2d, ln1_g, BF16)
    proj_m, wg_b = _matmul_nt(h1, w_in_t, BF16, "inproj_mlstm", n_out=n_m,
                              cast_src=w_ffn_gate)
    proj_g, wu_b = _matmul_nt(h1, w_g, BF16, "inproj_gla", cast_src=w_ffn_up)
    small = _matmul_nt(h1, w_small, F32, "inproj_small")

    L = 128
    gates = small[:, :2 * H].reshape(B, S // L, L, 2 * H)
    gates = jnp.transpose(gates, (0, 1, 3, 2))
    bias = jnp.concatenate([mlstm_i_b, mlstm_f_b]).astype(F32).reshape(2 * H, 1)
    w2p = jnp.zeros((SMALL_W, H * G_DK), F32).at[2 * H:2 * H + G_RANK].set(gla_a_w2.astype(F32))
    ym, yg, w_out_b = _scans(proj_m, gates, bias, mlstm_conv_w.astype(F32),
                             mlstm_conv_b.reshape(1, -1).astype(F32),
                             mlstm_norm_g.reshape(1, -1).astype(F32),
                             proj_g, small, w2p, gla_a_b.reshape(1, -1).astype(F32),
                             gla_norm_g.reshape(1, -1).astype(F32), w_out,
                             B=B, S=S, DK=M_DK, DV=M_DV, L=L)

    x1 = _outproj(ym, yg, w_out_b, x2d)

    h2 = _rmsnorm(x1, ln2_g, BF16)
    act, wd_b = _ffn_up(h2, wg_b, wu_b, ffn_conv_w.astype(F32),
                        ffn_conv_b.reshape(1, -1).astype(F32), w_ffn_down, S=S)
    out = _ffn_down(act, wd_b, x1, lnf_g.reshape(1, -1).astype(F32))
    return out.reshape(B, S, D)
```

```python
import functools

import jax
import jax.numpy as jnp
from jax import lax
from jax.experimental import pallas as pl
from jax.experimental.pallas import tpu as pltpu

F32 = jnp.float32
BF16 = jnp.bfloat16

HEADS = 4
QK_CONV = 4
FFN_CONV = 3
G_RANK = 16
G_TAU_INV = 1.0 / 16.0
EPS = 1e-6
SMALL_W = 128
HALO = 8
NEG_INF = float("-inf")
FFN_TN = 512
FFN_TK = 1024
MXU_COLS_V7X = 256
LANES = 128

VMEM_LIMIT_V7X = 56 * 1024 * 1024


def _cparams(sem, vmem=VMEM_LIMIT_V7X):
    return pltpu.CompilerParams(dimension_semantics=sem, vmem_limit_bytes=vmem)


def _pick(n, prefs):
    for p in prefs:
        if n % p == 0:
            return p
    raise ValueError(f"no tile in {prefs} divides {n}")


def _log_sigmoid(x):
    return jnp.minimum(x, 0.0) - jnp.log1p(jnp.exp(-jnp.abs(x)))


def _dot(a, b):
    return jnp.dot(a, b, preferred_element_type=F32)


def _dot_hi(a, b):
    return jnp.dot(a, b, preferred_element_type=F32, precision=lax.Precision.HIGHEST)


def _dot_nt(a, b):
    return lax.dot_general(a, b, (((1,), (1,)), ((), ())), preferred_element_type=F32)


def _dot_tn(a, b):
    return lax.dot_general(a, b, (((0,), (0,)), ((), ())), preferred_element_type=F32)


def _rms(x):
    return x * lax.rsqrt(jnp.mean(x * x, axis=-1, keepdims=True) + EPS)


def _rmsnorm_kernel(x_ref, g_ref, o_ref):
    o_ref[...] = (_rms(x_ref[...].astype(F32)) * g_ref[...]).astype(o_ref.dtype)


def _rmsnorm(x, g, out_dtype):
    T, D = x.shape
    tr = _pick(T, (512, 256, 128, 64, 8))
    return pl.pallas_call(
        _rmsnorm_kernel,
        grid=(T // tr,),
        in_specs=[pl.BlockSpec((tr, D), lambda i: (i, 0)),
                  pl.BlockSpec((1, D), lambda i: (0, 0))],
        out_specs=pl.BlockSpec((tr, D), lambda i: (i, 0)),
        out_shape=jax.ShapeDtypeStruct((T, D), out_dtype),
        compiler_params=_cparams(("parallel",)),
        name="rmsnorm",
    )(x, g.reshape(1, D).astype(F32))


def _cast_block(n, steps, min_block):
    cands = [c for c in range(LANES, n + 1, LANES) if n % c == 0 and n // c <= steps]
    wide = [c for c in cands if c >= min_block]
    return (wide or cands)[0]


def _matmul_nt_kernel(a_ref, wt_ref, *rest, n_cast):
    o_ref = rest[-2] if n_cast else rest[-1]
    o_ref[...] = _dot_nt(a_ref[...], wt_ref[...]).astype(o_ref.dtype)
    if n_cast:
        src_ref, dst_ref = rest[0], rest[-1]
        step = pl.program_id(0) * pl.num_programs(1) + pl.program_id(1)

        @pl.when(step < n_cast)
        def _():
            dst_ref[...] = src_ref[...].astype(dst_ref.dtype)


def _matmul_nt(a, wt, out_dtype, name, n_out=None, cast_src=None):
    M, K = a.shape
    N = wt.shape[0] if n_out is None else n_out
    tm = _pick(M, (1024, 512, 256, 128))
    tn = _pick(N, (1024, 512, 256, 128))
    grid = (M // tm, N // tn)
    in_specs = [pl.BlockSpec((tm, K), lambda i, j: (i, 0)),
                pl.BlockSpec((tn, K), lambda i, j: (j, 0))]
    out_specs = [pl.BlockSpec((tm, tn), lambda i, j: (i, j))]
    out_shape = [jax.ShapeDtypeStruct((M, N), out_dtype)]
    args = [a, wt]
    n_cast = 0
    if cast_src is not None:
        R, C = cast_src.shape
        cb = _cast_block(C, grid[0] * grid[1], 2 * LANES)
        n_cast = C // cb
        blk = pl.BlockSpec((R, cb), lambda i, j: (0, jnp.minimum(i * grid[1] + j, n_cast - 1)))
        in_specs.append(blk)
        out_specs.append(blk)
        out_shape.append(jax.ShapeDtypeStruct((R, C), BF16))
        args.append(cast_src)
    res = pl.pallas_call(
        functools.partial(_matmul_nt_kernel, n_cast=n_cast),
        grid=grid,
        in_specs=in_specs,
        out_specs=out_specs,
        out_shape=out_shape,
        compiler_params=_cparams(("arbitrary", "arbitrary")),
        name=name,
    )(*args)
    return res if n_cast else res[0]


def _outproj_kernel(ym_ref, yg_ref, wt_ref, wb_ref, x_ref, o_ref):
    acc = _dot(ym_ref[...], wt_ref[...]) + _dot(yg_ref[...], wb_ref[...])
    o_ref[...] = x_ref[...] + acc


def _outproj(ym, yg, w, x):
    M, K2 = ym.shape
    N = w.shape[1]
    tm = _pick(M, (1024, 512, 256, 128))
    tn = _pick(N, (1024, 512, 256, 128))
    return pl.pallas_call(
        _outproj_kernel,
        grid=(M // tm, N // tn),
        in_specs=[pl.BlockSpec((tm, K2), lambda i, j: (i, 0)),
                  pl.BlockSpec((tm, K2), lambda i, j: (i, 0)),
                  pl.BlockSpec((K2, tn), lambda i, j: (0, j)),
                  pl.BlockSpec((K2, tn), lambda i, j: (1, j)),
                  pl.BlockSpec((tm, tn), lambda i, j: (i, j))],
        out_specs=pl.BlockSpec((tm, tn), lambda i, j: (i, j)),
        out_shape=jax.ShapeDtypeStruct((M, N), F32),
        compiler_params=_cparams(("parallel", "arbitrary")),
        name="outproj",
    )(ym, yg, w, w, x)


def _conv_silu(raw_ref, x_ref, shift_ref, w_ref, b_ref, L):
    raw = raw_ref[...]
    x_ref[L:2 * L, :] = raw
    acc = b_ref[...] + w_ref[QK_CONV - 1:QK_CONV, :] * raw.astype(F32)
    for j in range(QK_CONV - 1):
        acc = acc + w_ref[j:j + 1, :] * _dot(shift_ref[j], x_ref[...])
    x_ref[0:L, :] = raw
    return acc * jax.nn.sigmoid(acc)


def _row_to_col(row, eye):
    L = eye.shape[0]
    return jnp.sum(jnp.where(eye, jnp.broadcast_to(row, (L, L)), 0.0), axis=1, keepdims=True)


def _mlstm_init(C_ref, n_ref, m_ref, xq_ref, xk_ref):
    L = xq_ref.shape[0] // 2
    C_ref[...] = jnp.zeros_like(C_ref)
    n_ref[...] = jnp.zeros_like(n_ref)
    m_ref[...] = jnp.zeros_like(m_ref)
    xq_ref[0:L, :] = jnp.zeros((L, xq_ref.shape[1]), BF16)
    xk_ref[0:L, :] = jnp.zeros((L, xk_ref.shape[1]), BF16)


def _mlstm_body(bias_ref, q_ref, k_ref, v_ref, o_ref, g_ref, sh_ref, cwq_ref, cwk_ref, cbq_ref,
                cbk_ref, ng_ref, out_ref, C_ref, n_ref, m_ref, xq_ref, xk_ref, *, L, DK, DV):
    H = HEADS
    q_all = _conv_silu(q_ref, xq_ref, sh_ref, cwq_ref, cbq_ref, L) * (DK ** -0.5)
    k_all = _conv_silu(k_ref, xk_ref, sh_ref, cwk_ref, cbk_ref, L)

    G = g_ref[...] + bias_ref[...]
    bc = _log_sigmoid(G)
    lane = lax.broadcasted_iota(jnp.int32, (2 * H, L), 1)
    s = 1
    while s < L:
        bc = bc + jnp.where(lane >= s, pltpu.roll(bc, s, axis=1), 0.0)
        s *= 2

    row = lax.broadcasted_iota(jnp.int32, (L, L), 0)
    col = lax.broadcasted_iota(jnp.int32, (L, L), 1)
    eye = row == col
    causal = col <= row

    for h in range(H):
        q = q_all[:, h * DK:(h + 1) * DK]
        k = k_all[:, h * DK:(h + 1) * DK]
        v = v_ref[:, h * DV:(h + 1) * DV]
        gi = G[h:h + 1, :]
        b_row = bc[H + h:H + h + 1, :]
        b_col = _row_to_col(b_row, eye)

        m_prev = m_ref[h]
        C = C_ref[h]
        n = n_ref[h]
        D = jnp.where(causal, b_col - b_row + gi, NEG_INF)
        inter = b_col + m_prev
        m_t = jnp.maximum(inter, jnp.max(D, axis=1, keepdims=True))
        g = jnp.exp(inter - m_t)
        qb = q.astype(BF16)
        kb = k.astype(BF16)
        s_qk = _dot_nt(qb, kb) * jnp.exp(D - m_t)
        num = g * _dot(qb, C.astype(BF16)) + _dot(s_qk.astype(BF16), v)
        den = (g * jnp.sum(q * n, axis=1, keepdims=True)
               + jnp.sum(s_qk, axis=1, keepdims=True))
        hh = num / jnp.maximum(jnp.abs(den), jnp.exp(-m_t))

        y = (_rms(hh) * ng_ref[:, h * DV:(h + 1) * DV]
             * jax.nn.sigmoid(o_ref[:, h * DV:(h + 1) * DV]).astype(F32))
        out_ref[:, h * DV:(h + 1) * DV] = y.astype(out_ref.dtype)

        bL = b_row[:, L - 1:L]
        a_row = bL - b_row + gi
        m_new = jnp.maximum(bL + m_prev, jnp.max(a_row, axis=1, keepdims=True))
        ws_col = _row_to_col(jnp.exp(a_row - m_new), eye)
        gC = jnp.exp(bL + m_prev - m_new)
        kw = k * ws_col
        C_ref[h] = gC * C + _dot_tn(kw.astype(BF16), v)
        n_ref[h] = gC * n + jnp.sum(kw, axis=0, keepdims=True)
        m_ref[h] = m_new


def _gla_init(ST_ref, bs_ref):
    ST_ref[...] = jnp.zeros_like(ST_ref)
    bs_ref[...] = jnp.zeros_like(bs_ref)


def _gla_body(q_ref, k_ref, v_ref, gg_ref, sm_ref, w2_ref, ab_ref, ng_ref, out_ref,
              ST_ref, bs_ref, *, L, DK, DV):
    H = HEADS
    z = _dot_hi(sm_ref[...], w2_ref[...]) + ab_ref[...]
    la = _log_sigmoid(z) * G_TAU_INV
    row = lax.broadcasted_iota(jnp.int32, (L, L), 0)
    col = lax.broadcasted_iota(jnp.int32, (L, L), 1)
    tril = jnp.where(col <= row, 1.0, 0.0).astype(F32)
    b_all = _dot_hi(tril, la)
    bs_ref[HALO:HALO + L, :] = b_all
    pos_col = lax.broadcasted_iota(jnp.int32, (L, DK), 0)
    eye = row == col

    for h in range(H):
        dk = slice(h * DK, (h + 1) * DK)
        dv = slice(h * DV, (h + 1) * DV)
        b = b_all[:, dk]
        q = q_ref[:, dk].astype(F32) * (DK ** -0.5)
        k = k_ref[:, dk].astype(F32)
        v = v_ref[:, dv]
        ST = ST_ref[h]
        o = _dot((q * jnp.exp(b)).astype(BF16), ST.astype(BF16))

        A = jnp.where(eye, _dot_nt(q.astype(BF16), k.astype(BF16)), 0.0)
        n = L
        while n >= 2:
            hn = n // 2
            pos = pos_col % n
            if n >= 8:
                R = jnp.concatenate(
                    [jnp.broadcast_to(b[i * n + hn - 1:i * n + hn, :], (n, DK))
                     for i in range(L // n)], axis=0)
            elif n == 4:
                R = jnp.where(pos == 0, bs_ref[HALO + 1:HALO + 1 + L, dk],
                              jnp.where(pos == 1, b,
                                        jnp.where(pos == 2, bs_ref[HALO - 1:HALO - 1 + L, dk],
                                                  bs_ref[HALO - 2:HALO - 2 + L, dk])))
            else:
                R = jnp.where(pos == 0, b, bs_ref[HALO - 1:HALO - 1 + L, dk])
            qn = q * jnp.exp(jnp.where(pos >= hn, b - R, NEG_INF))
            kn = k * jnp.exp(jnp.where(pos < hn, R - b, NEG_INF))
            S = _dot_nt(qn.astype(BF16), kn.astype(BF16))
            A = A + jnp.where((row // n) == (col // n), S, 0.0)
            n = hn
        o = o + _dot(A.astype(BF16), v)

        gg = gg_ref[:, dv]
        y = _rms(o) * ng_ref[:, dv] * (gg * jax.nn.sigmoid(gg)).astype(F32)
        out_ref[:, dv] = y.astype(out_ref.dtype)

        bL = b[L - 1:L, :]
        kdec = k * jnp.exp(bL - b)
        decay_col = jnp.transpose(jnp.broadcast_to(jnp.exp(bL), (LANES, DK)))[:, 0:1]
        ST_ref[h] = ST * decay_col + _dot_tn(kdec.astype(BF16), v)


N_MLSTM_IN, N_GLA_IN, N_MLSTM_SCRATCH = 12, 8, 5


def _scan_kernel(*refs, L, DK, DV, n_cast):
    n_in = N_MLSTM_IN + N_GLA_IN
    m_in = refs[:N_MLSTM_IN]
    g_in = refs[N_MLSTM_IN:n_in]
    src_ref = refs[n_in]
    ym_ref, yg_ref, dst_ref = refs[n_in + 1:n_in + 4]
    scratch = refs[n_in + 4:]
    m_scr, g_scr = scratch[:N_MLSTM_SCRATCH], scratch[N_MLSTM_SCRATCH:]

    @pl.when(pl.program_id(1) == 0)
    def _():
        _mlstm_init(*m_scr)
        _gla_init(*g_scr)

    @pl.when(pl.program_id(0) * pl.num_programs(1) + pl.program_id(1) < n_cast)
    def _():
        dst_ref[...] = src_ref[...].astype(dst_ref.dtype)

    _mlstm_body(*m_in, ym_ref, *m_scr, L=L, DK=DK, DV=DV)
    _gla_body(*g_in, yg_ref, *g_scr, L=L, DK=DK, DV=DV)


def _scans(proj_m, gates, bias, conv_w, conv_b, norm_m, proj_g, small, w2p, a_b, norm_g, w_cast,
           *, B, S, DK, DV, L):
    H = HEADS
    NC = S // L
    T = B * S
    rb = _cast_block(w_cast.shape[0], B * NC, LANES)
    n_cast = w_cast.shape[0] // rb
    cast_blk = pl.BlockSpec((rb, w_cast.shape[1]),
                            lambda b, c: (jnp.minimum(b * NC + c, n_cast - 1), 0))
    kern = functools.partial(_scan_kernel, L=L, DK=DK, DV=DV, n_cast=n_cast)
    rowblk = lambda b, c: b * NC + c
    t_idx = jnp.arange(L)[None, :, None]
    r_idx = jnp.arange(2 * L)[None, None, :]
    lag = (QK_CONV - 1 - jnp.arange(QK_CONV - 1))[:, None, None]
    shifts = (r_idx == L + t_idx - lag).astype(BF16)
    qk = lambda j: pl.BlockSpec((L, H * DK), lambda b, c: (rowblk(b, c), j))
    vo = lambda j: pl.BlockSpec((L, H * DV), lambda b, c: (rowblk(b, c), j))
    fixed = lambda shape: pl.BlockSpec(shape, lambda b, c: (0,) * len(shape))
    out_spec = pl.BlockSpec((L, H * DV), lambda b, c: (rowblk(b, c), 0))
    out_sds = jax.ShapeDtypeStruct((T, H * DV), BF16)
    return pl.pallas_call(
        kern,
        grid=(B, NC),
        in_specs=[
            fixed((2 * H, 1)),
            qk(0), qk(1), vo(1), vo(2),
            pl.BlockSpec((None, None, 2 * H, L), lambda b, c: (b, c, 0, 0)),
            fixed((QK_CONV - 1, L, 2 * L)),
            pl.BlockSpec((QK_CONV, H * DK), lambda b, c: (0, 0)),
            pl.BlockSpec((QK_CONV, H * DK), lambda b, c: (0, 1)),
            pl.BlockSpec((1, H * DK), lambda b, c: (0, 0)),
            pl.BlockSpec((1, H * DK), lambda b, c: (0, 1)),
            fixed((1, H * DV)),
            qk(0), qk(1), vo(1), vo(2),
            pl.BlockSpec((L, SMALL_W), lambda b, c: (rowblk(b, c), 0)),
            fixed((SMALL_W, H * DK)),
            fixed((1, H * DK)),
            fixed((1, H * DV)),
            cast_blk,
        ],
        out_specs=[out_spec, out_spec, cast_blk],
        out_shape=[out_sds, out_sds, jax.ShapeDtypeStruct(w_cast.shape, BF16)],
        scratch_shapes=[
            pltpu.VMEM((H, DK, DV), F32),
            pltpu.VMEM((H, 1, DK), F32),
            pltpu.VMEM((H, 1, 1), F32),
            pltpu.VMEM((2 * L, H * DK), BF16),
            pltpu.VMEM((2 * L, H * DK), BF16),
            pltpu.VMEM((H, DK, DV), F32),
            pltpu.VMEM((L + 2 * HALO, H * DK), F32),
        ],
        compiler_params=_cparams(("arbitrary", "arbitrary")),
        name="scans",
    )(bias, proj_m, proj_m, proj_m, proj_m, gates, shifts, conv_w, conv_w, conv_b, conv_b, norm_m,
      proj_g, proj_g, proj_g, proj_g, small, w2p, a_b, norm_g, w_cast)


def _ffn_up_kernel(h_ref, wg_ref, wu_ref, cw_ref, cb_ref, wd_ref, act_ref, wdb_ref, gbuf_ref, ghalo_ref,
                   *, tm, S, tg, groups_last, n_cast):
    i = pl.program_id(0)
    j = pl.program_id(1)
    nj = pl.num_programs(1)
    ng = act_ref.shape[1] // tg
    seq_start = (i * tm) % S == 0

    @pl.when(jnp.logical_and(i == 0, j == 0))
    def _():
        ghalo_ref[...] = jnp.zeros_like(ghalo_ref)

    @pl.when(i * nj + j < n_cast)
    def _():
        wdb_ref[...] = wd_ref[...].astype(wdb_ref.dtype)

    def body(n_groups):
        hb = h_ref[...]
        silu = []
        for g in range(n_groups):
            cols = slice(g * tg, (g + 1) * tg)
            gate = _dot(hb, wg_ref[:, cols])
            gbuf_ref[g, 0:HALO, :] = jnp.where(seq_start, 0.0, ghalo_ref[j, :, cols])
            gbuf_ref[g, HALO:HALO + tm, :] = gate
            ghalo_ref[j, :, cols] = gate[tm - HALO:tm, :]
            conv = cb_ref[:, cols] + cw_ref[FFN_CONV - 1:FFN_CONV, cols] * gate
            for t in range(FFN_CONV - 1):
                off = HALO - (FFN_CONV - 1) + t
                conv = conv + cw_ref[t:t + 1, cols] * gbuf_ref[g, off:off + tm, :]
            silu.append(conv * jax.nn.sigmoid(conv))
        for g in range(n_groups):
            cols = slice(g * tg, (g + 1) * tg)
            up = _dot(hb, wu_ref[:, cols])
            act_ref[:, cols] = (silu[g] * up).astype(act_ref.dtype)

    if groups_last == ng:
        body(ng)
    else:
        pl.when(j < nj - 1)(functools.partial(body, ng))
        pl.when(j == nj - 1)(functools.partial(body, groups_last))


def _ffn_up(h2, wg, wu, conv_w, conv_b, wd, *, S):
    T, D = h2.shape
    F = wg.shape[1]
    tm = _pick(S, (1024, 512, 256, 128))
    tg = _pick(F, (MXU_COLS_V7X, 128))
    tn = FFN_TN
    nj = -(-F // tn)
    groups_last = (F - (nj - 1) * tn) // tg
    rb = _cast_block(F, (T // tm) * nj, LANES)
    n_cast = F // rb
    kern = functools.partial(_ffn_up_kernel, tm=tm, S=S, tg=tg, groups_last=groups_last,
                             n_cast=n_cast)
    wd_blk = pl.BlockSpec((rb, D), lambda i, j: (jnp.minimum(i * nj + j, n_cast - 1), 0))
    return pl.pallas_call(
        kern,
        grid=(T // tm, nj),
        in_specs=[
            pl.BlockSpec((tm, D), lambda i, j: (i, 0)),
            pl.BlockSpec((D, tn), lambda i, j: (0, j)),
            pl.BlockSpec((D, tn), lambda i, j: (0, j)),
            pl.BlockSpec((FFN_CONV, tn), lambda i, j: (0, j)),
            pl.BlockSpec((1, tn), lambda i, j: (0, j)),
            wd_blk,
        ],
        out_specs=[pl.BlockSpec((tm, tn), lambda i, j: (i, j)), wd_blk],
        out_shape=[jax.ShapeDtypeStruct((T, F), BF16), jax.ShapeDtypeStruct((F, D), BF16)],
        scratch_shapes=[pltpu.VMEM((tn // tg, tm + HALO, tg), F32),
                        pltpu.VMEM((nj, HALO, tn), F32)],
        compiler_params=_cparams(("arbitrary", "arbitrary")),
        name="ffn_up",
    )(h2, wg, wu, conv_w, conv_b, wd)


def _ffn_down_kernel(a_ref, wd_ref, x_ref, gf_ref, out_ref, *, tn, k_tail):
    k = pl.program_id(1)
    nk = pl.num_programs(1)
    tk = a_ref.shape[1]
    D = out_ref.shape[1]

    def accumulate(kk, first):
        a = a_ref[:, :kk]
        for n0 in range(0, D, tn):
            d = _dot(a, wd_ref[:kk, n0:n0 + tn])
            if first:
                out_ref[:, n0:n0 + tn] = x_ref[:, n0:n0 + tn] + d
            else:
                out_ref[:, n0:n0 + tn] += d

    pl.when(k == 0)(functools.partial(accumulate, tk, True))
    pl.when(jnp.logical_and(k > 0, k < nk - 1))(functools.partial(accumulate, tk, False))
    pl.when(k == nk - 1)(functools.partial(accumulate, k_tail, False))

    @pl.when(k == nk - 1)
    def _():
        out_ref[...] = _rms(out_ref[...]) * gf_ref[...]


def _ffn_down(act, wd, x1, lnf_g):
    T, F = act.shape
    D = wd.shape[1]
    tm = _pick(T, (512, 256, 128))
    tn = _pick(D, (1024, 512, 256, 128))
    tk = FFN_TK
    nk = -(-F // tk)
    assert nk >= 2
    kern = functools.partial(_ffn_down_kernel, tn=tn, k_tail=F - (nk - 1) * tk)
    return pl.pallas_call(
        kern,
        grid=(T // tm, nk),
        in_specs=[
            pl.BlockSpec((tm, tk), lambda i, k: (i, k)),
            pl.BlockSpec((tk, D), lambda i, k: (k, 0)),
            pl.BlockSpec((tm, D), lambda i, k: (i, 0)),
            pl.BlockSpec((1, D), lambda i, k: (0, 0)),
        ],
        out_specs=pl.BlockSpec((tm, D), lambda i, k: (i, 0)),
        out_shape=jax.ShapeDtypeStruct((T, D), F32),
        compiler_params=_cparams(("parallel", "arbitrary")),
        name="ffn_down",
    )(act, wd, x1, lnf_g)


def kernel(x, ln1_g, w_in, mlstm_conv_w, mlstm_conv_b, mlstm_i_b, mlstm_f_b, mlstm_norm_g,
           gla_a_w2, gla_a_b, gla_norm_g, w_out, ln2_g, w_ffn_gate, w_ffn_up,
           ffn_conv_w, ffn_conv_b, w_ffn_down, lnf_g):
    B, S, D = x.shape
    T = B * S
    H = HEADS
    M_DK = mlstm_conv_w.shape[1] // (2 * H)
    M_DV = mlstm_norm_g.shape[0] // H
    G_DK = gla_a_w2.shape[1] // H
    G_DV = gla_norm_g.shape[0] // H
    F = w_ffn_gate.shape[1]
    assert M_DV == 2 * M_DK and G_DV == 2 * G_DK and M_DK == G_DK

    n_m = 2 * H * M_DK + 2 * H * M_DV
    n_g = 2 * H * G_DK + 2 * H * G_DV
    o_gate = n_m
    o_g = n_m + 2 * H
    o_a = o_g + n_g
    assert w_in.shape[1] == o_a + G_RANK

    w_in_t = w_in.T.astype(BF16)
    w_g = w_in_t[o_g:o_a]
    w_small = jnp.concatenate(
        [w_in_t[o_gate:o_g], w_in_t[o_a:],
         jnp.zeros((SMALL_W - 2 * H - G_RANK, D), BF16)], axis=0)

    x2d = x.reshape(T, D)
    h1 = _rmsnorm(x2d, ln1_g, BF16)
    proj_m, wg_b = _matmul_nt(h1, w_in_t, BF16, "inproj_mlstm", n_out=n_m,
                              cast_src=w_ffn_gate)
    proj_g, wu_b = _matmul_nt(h1, w_g, BF16, "inproj_gla", cast_src=w_ffn_up)
    small = _matmul_nt(h1, w_small, F32, "inproj_small")

    L = 128
    gates = small[:, :2 * H].reshape(B, S // L, L, 2 * H)
    gates = jnp.transpose(gates, (0, 1, 3, 2))
    bias = jnp.concatenate([mlstm_i_b, mlstm_f_b]).astype(F32).reshape(2 * H, 1)
    w2p = jnp.zeros((SMALL_W, H * G_DK), F32).at[2 * H:2 * H + G_RANK].set(gla_a_w2.astype(F32))
    ym, yg, w_out_b = _scans(proj_m, gates, bias, mlstm_conv_w.astype(F32),
                             mlstm_conv_b.reshape(1, -1).astype(F32),
                             mlstm_norm_g.reshape(1, -1).astype(F32),
                             proj_g, small, w2p, gla_a_b.reshape(1, -1).astype(F32),
                             gla_norm_g.reshape(1, -1).astype(F32), w_out,
                             B=B, S=S, DK=M_DK, DV=M_DV, L=L)

    x1 = _outproj(ym, yg, w_out_b, x2d)

    h2 = _rmsnorm(x1, ln2_g, BF16)
    act, wd_b = _ffn_up(h2, wg_b, wu_b, ffn_conv_w.astype(F32),
                        ffn_conv_b.reshape(1, -1).astype(F32), w_ffn_down, S=S)
    out = _ffn_down(act, wd_b, x1, lnf_g.reshape(1, -1).astype(F32))
    return out.reshape(B, S, D)
```
